```python
import math
import jax, jax.numpy as jnp
from jax import lax
import numpy as np

D_MODEL = 1024
BATCH = 2
SEQ = 8192
DEPTH = 2

HEAD_DIM = 64
N_HEADS_A = 8
N_HEADS_B = 8
WIDTH_A = N_HEADS_A * HEAD_DIM
WIDTH_B = N_HEADS_B * HEAD_DIM
MIX_WIDTH = WIDTH_A + WIDTH_B
IN_SIZES = (WIDTH_A, WIDTH_A, WIDTH_A, WIDTH_B, WIDTH_B, WIDTH_B, MIX_WIDTH)
IN_COLS = sum(IN_SIZES)
DILATED_CONFIGS = ((128, 1), (512, 4), (2048, 16))
BLK = 128
N_BUCKETS = 32
MAX_DISTANCE = 2048
LN_EPS = 1e-5
DEEPNORM_ALPHA = (2.0 * DEPTH) ** 0.25
DEEPNORM_BETA = (8.0 * DEPTH) ** -0.25

kernel_name = "hybrid_dilated_stickbreaking_deepnorm"


def layer_norm(x, g, b):
    xf = x.astype(jnp.float32)
    mu = jnp.mean(xf, axis=-1, keepdims=True)
    var = jnp.mean(jnp.square(xf - mu), axis=-1, keepdims=True)
    y = (xf - mu) * lax.rsqrt(var + LN_EPS) * g.astype(jnp.float32) + b.astype(jnp.float32)
    return y.astype(x.dtype)


def t5_bucket(dist):
    max_exact = N_BUCKETS // 2
    n = np.maximum(dist, 1).astype(np.float32)
    large = max_exact + (np.log(n / max_exact) / math.log(MAX_DISTANCE / max_exact)
                         * (N_BUCKETS - max_exact)).astype(np.int32)
    large = np.minimum(large, N_BUCKETS - 1)
    return np.where(dist < max_exact, dist, large).astype(np.int32)


def dilated_branch(q, k, v, rel_bias, window, dil):
    b, t, h, dh = q.shape
    wc = window // dil
    tc = t // dil
    nb = -(-tc // BLK)
    pad = nb * BLK - tc

    def fold(a):
        a = a.reshape(b, tc, dil, h, dh).transpose(0, 2, 1, 3, 4)
        a = jnp.pad(a, ((0, 0), (0, 0), (0, pad), (0, 0), (0, 0)))
        return a.reshape(b, dil, nb, BLK, h, dh)

    def with_prev(a):
        prev = jnp.pad(a[:, :, :-1], ((0, 0), (0, 0), (1, 0), (0, 0), (0, 0), (0, 0)))
        return jnp.concatenate([prev, a], axis=3)

    def unfold(a):
        trailing = a.shape[4:]
        a = a.reshape((b, dil, nb * BLK) + trailing)[:, :, :tc]
        return jnp.swapaxes(a, 1, 2).reshape((b, t) + trailing)

    qf = fold(q)
    kf = with_prev(fold(k))
    vf = with_prev(fold(v))

    i = np.arange(BLK)[:, None]
    j = np.arange(2 * BLK)[None, :]
    delta = BLK + i - j
    band = (delta >= 0) & (delta <= wc)
    valid = band[None] & ((np.arange(nb)[:, None, None] > 0) | (j[None] >= BLK))
    bucket = t5_bucket(np.clip(delta, 0, wc) * dil)
    bias = jnp.transpose(rel_bias[bucket], (2, 0, 1)).astype(jnp.float32)

    s = jnp.einsum('brnqhd,brnkhd->brnhqk', qf, kf).astype(jnp.float32) / math.sqrt(dh) + bias
    s = jnp.where(valid[:, None], s, -jnp.inf)
    m = jnp.max(s, axis=-1, keepdims=True)
    p = jnp.exp(s - m)
    den = jnp.sum(p, axis=-1)
    o = jnp.einsum('brnhqk,brnkhd->brnqhd', p.astype(v.dtype), vf).astype(jnp.float32)
    o = o / jnp.swapaxes(den, -1, -2)[..., None]
    lse = jnp.swapaxes(m[..., 0] + jnp.log(den), -1, -2)
    return unfold(o), unfold(lse)


def dilated_mixture(q, k, v, rel_bias):
    outs, lses = [], []
    for window, dil in DILATED_CONFIGS:
        o, lse = dilated_branch(q, k, v, rel_bias, window, dil)
        outs.append(o)
        lses.append(lse)
    w = jax.nn.softmax(jnp.stack(lses, axis=0), axis=0)
    y = jnp.sum(w[..., None] * jnp.stack(outs, axis=0), axis=0)
    return y.astype(q.dtype)


def stick_breaking(q, k, v):
    b, t, h, dh = q.shape
    nb = t // BLK
    qb = q.reshape(b, nb, BLK, h, dh).transpose(1, 0, 2, 3, 4)
    starts = jnp.arange(nb, dtype=jnp.int32) * BLK
    kpos = jnp.arange(t, dtype=jnp.int32)
    scale = 1.0 / math.sqrt(dh)

    def one_block(args):
        qblk, start = args
        z = jnp.einsum('bqhd,bkhd->bhqk', qblk, k).astype(jnp.float32) * scale
        qpos = start + jnp.arange(BLK, dtype=jnp.int32)
        causal = kpos[None, :] < qpos[:, None]
        log_beta = jax.nn.log_sigmoid(z)
        log1m = jnp.where(causal, jax.nn.log_sigmoid(-z), 0.0)
        after = lax.cumsum(log1m, axis=3, reverse=True) - log1m
        a = jnp.where(causal, jnp.exp(log_beta + after), 0.0)
        return jnp.einsum('bhqk,bkhd->bqhd', a.astype(v.dtype), v)

    out = lax.map(one_block, (qb, starts))
    return out.transpose(1, 0, 2, 3, 4).reshape(b, t, h, dh)


def setup_inputs(seed: int = 0) -> dict:
    key = jax.random.key(seed)
    ks = jax.random.split(key, 10)
    x = jax.random.normal(ks[0], (BATCH, SEQ, D_MODEL), jnp.float32)
    ln0_g = 1.0 + 0.02 * jax.random.normal(ks[1], (D_MODEL,), jnp.float32)
    ln0_b = 0.02 * jax.random.normal(ks[2], (D_MODEL,), jnp.float32)
    col_scale = np.ones((IN_COLS,), np.float32)
    offs = np.concatenate([[0], np.cumsum(IN_SIZES)])
    col_scale[offs[2]:offs[3]] = DEEPNORM_BETA
    col_scale[offs[5]:offs[6]] = DEEPNORM_BETA
    w_in = (jax.random.normal(ks[3], (DEPTH, D_MODEL, IN_COLS), jnp.float32)
            * (D_MODEL ** -0.5) * jnp.asarray(col_scale))
    w_out = (jax.random.normal(ks[4], (DEPTH, MIX_WIDTH, D_MODEL), jnp.float32)
             * (MIX_WIDTH ** -0.5) * DEEPNORM_BETA)
    b_out = 0.02 * jax.random.normal(ks[5], (DEPTH, D_MODEL), jnp.float32)
    ln_g = 1.0 + 0.02 * jax.random.normal(ks[6], (DEPTH, D_MODEL), jnp.float32)
    ln_b = 0.02 * jax.random.normal(ks[7], (DEPTH, D_MODEL), jnp.float32)
    rel_bias = 0.5 * jax.random.normal(ks[8], (N_BUCKETS, N_HEADS_A), jnp.float32)
    return {"x": x, "ln0_g": ln0_g, "ln0_b": ln0_b, "w_in": w_in, "w_out": w_out,
            "b_out": b_out, "ln_g": ln_g, "ln_b": ln_b, "rel_bias": rel_bias}


def reference(x, ln0_g, ln0_b, w_in, w_out, b_out, ln_g, ln_b, rel_bias):
    b, t, _ = x.shape
    split_at = [int(s) for s in np.cumsum(IN_SIZES)[:-1]]
    h = layer_norm(x, ln0_g, ln0_b)
    for l in range(DEPTH):
        proj = jnp.einsum('btd,dc->btc', h, w_in[l])
        qa, ka, va, qb, kb, vb, gate = jnp.split(proj, split_at, axis=-1)
        heads_a = lambda a: a.reshape(b, t, N_HEADS_A, HEAD_DIM)
        heads_b = lambda a: a.reshape(b, t, N_HEADS_B, HEAD_DIM)
        ya = dilated_mixture(heads_a(qa), heads_a(ka), heads_a(va), rel_bias).reshape(b, t, WIDTH_A)
        yb = stick_breaking(heads_b(qb), heads_b(kb), heads_b(vb)).reshape(b, t, WIDTH_B)
        y = jnp.concatenate([ya, yb], axis=-1) * jax.nn.silu(gate)
        out = jnp.einsum('btc,cd->btd', y, w_out[l]) + b_out[l]
        h = layer_norm(DEEPNORM_ALPHA * h + out, ln_g[l], ln_b[l])
    return h
```

```python
import functools
import math

import numpy as np
import jax
import jax.numpy as jnp
from jax import lax
from jax.experimental import pallas as pl
from jax.experimental.pallas import tpu as pltpu

D_MODEL = 1024
DEPTH = 2
HEAD_DIM = 64
N_HEADS_A = 8
N_HEADS_B = 8
WIDTH_A = N_HEADS_A * HEAD_DIM
WIDTH_B = N_HEADS_B * HEAD_DIM
MIX_WIDTH = WIDTH_A + WIDTH_B
IN_COLS = 3 * WIDTH_A + 3 * WIDTH_B + MIX_WIDTH
DILATED_CONFIGS = ((128, 1), (512, 4), (2048, 16))
BLK = 128
N_BUCKETS = 32
MAX_DISTANCE = 2048
LN_EPS = 1e-5
DEEPNORM_ALPHA = (2.0 * DEPTH) ** 0.25
QK_SCALE = 1.0 / math.sqrt(HEAD_DIM)

LANES = 128
PAIRS_A = WIDTH_A // LANES
PAIRS_B = WIDTH_B // LANES
NEG_BIG = -1e30

ROW_TILE = 512
SB_BLOCK = 256
VMEM_LIMIT = 48 * 1024 * 1024

F32 = jnp.float32
BF16 = jnp.bfloat16


def _params(semantics):
    return pltpu.CompilerParams(dimension_semantics=semantics, vmem_limit_bytes=VMEM_LIMIT)


def _layer_norm_rows(x, g, b):
    mu = jnp.mean(x, axis=-1, keepdims=True)
    xc = x - mu
    var = jnp.mean(xc * xc, axis=-1, keepdims=True)
    return xc * lax.rsqrt(var + LN_EPS) * g + b


def _ln0_kernel(x_ref, g_ref, b_ref, h_ref, hb_ref):
    y = _layer_norm_rows(x_ref[...], g_ref[...], b_ref[...])
    h_ref[...] = y
    hb_ref[...] = y.astype(BF16)


def _ln0(x2, g, b):
    n = x2.shape[0]
    row = pl.BlockSpec((ROW_TILE, D_MODEL), lambda i: (i, 0))
    vec = pl.BlockSpec((1, D_MODEL), lambda i: (0, 0))
    return pl.pallas_call(
        _ln0_kernel,
        grid=(n // ROW_TILE,),
        in_specs=[row, vec, vec],
        out_specs=[row, row],
        out_shape=[jax.ShapeDtypeStruct((n, D_MODEL), F32), jax.ShapeDtypeStruct((n, D_MODEL), BF16)],
        compiler_params=_params(("parallel",)),
        name="ln0",
    )(x2, g.reshape(1, D_MODEL), b.reshape(1, D_MODEL))


_PROJ_OUTS = (
    (0, WIDTH_A, QK_SCALE), (WIDTH_A, WIDTH_A, 1.0), (2 * WIDTH_A, WIDTH_A, 1.0),
    (3 * WIDTH_A, WIDTH_B, QK_SCALE), (3 * WIDTH_A + WIDTH_B, WIDTH_B, 1.0),
    (3 * WIDTH_A + 2 * WIDTH_B, WIDTH_B, 1.0), (3 * WIDTH_A + 3 * WIDTH_B, MIX_WIDTH, 1.0),
)


def _proj_kernel(h_ref, w_ref, *out_refs):
    h = h_ref[...]
    for (c0, width, scale), o_ref in zip(_PROJ_OUTS, out_refs):
        acc = jnp.dot(h, w_ref[:, c0:c0 + width], preferred_element_type=F32)
        if scale != 1.0:
            acc = acc * scale
        o_ref[...] = acc.astype(BF16)


def _project(hb, w_bf16):
    n = hb.shape[0]
    return pl.pallas_call(
        _proj_kernel,
        grid=(n // ROW_TILE,),
        in_specs=[pl.BlockSpec((ROW_TILE, D_MODEL), lambda i: (i, 0)),
                  pl.BlockSpec((D_MODEL, IN_COLS), lambda i: (0, 0))],
        out_specs=[pl.BlockSpec((ROW_TILE, w), lambda i: (i, 0)) for _, w, _ in _PROJ_OUTS],
        out_shape=[jax.ShapeDtypeStruct((n, w), BF16) for _, w, _ in _PROJ_OUTS],
        compiler_params=_params(("parallel",)),
        name="in_proj",
    )(hb, w_bf16)


def _split_pair(x_bf16):
    lane = lax.broadcasted_iota(jnp.int32, x_bf16.shape, 1)
    x = x_bf16.astype(F32)
    lo = jnp.where(lane < HEAD_DIM, x, 0.0).astype(BF16)
    hi = jnp.where(lane >= HEAD_DIM, x, 0.0).astype(BF16)
    return lo, hi


def _merge_pair(x0, x1):
    lane = lax.broadcasted_iota(jnp.int32, x0.shape, 1)
    return jnp.where(lane < HEAD_DIM, x0, x1)


_NT = (((1,), (1,)), ((), ()))


def _t5_bucket(dist):
    max_exact = N_BUCKETS // 2
    n = np.maximum(dist, 1).astype(np.float32)
    large = max_exact + (np.log(n / max_exact) / math.log(MAX_DISTANCE / max_exact)
                         * (N_BUCKETS - max_exact)).astype(np.int32)
    large = np.minimum(large, N_BUCKETS - 1)
    return np.where(dist < max_exact, dist, large).astype(np.int32)


def _bias_tables(rel_bias):
    i = np.arange(BLK)[:, None]
    j = np.arange(2 * BLK)[None, :]
    tabs = []
    for window, dil in DILATED_CONFIGS:
        wc = window // dil
        delta = BLK + i - j
        band = (delta >= 0) & (delta <= wc)
        bucket = _t5_bucket(np.clip(delta, 0, wc) * dil)
        bias = jnp.transpose(rel_bias[bucket], (2, 0, 1)).astype(F32)
        full = jnp.where(band[None], bias, NEG_BIG)
        first = jnp.concatenate([full[:, :, BLK:], jnp.full_like(full[:, :, BLK:], NEG_BIG)], axis=2)
        tabs.append(jnp.stack([full, first], axis=0))
    return jnp.stack(tabs, axis=0)


def _dilated_kernel(q_ref, k_ref, v_ref, bias_ref, o_ref, lse_ref, *, blocks_per_step):
    c = pl.program_id(3)

    def block(n, carry):
        g = c * blocks_per_step + n
        first = (g == 0).astype(jnp.int32)
        ks = pl.multiple_of(jnp.maximum(g - 1, 0) * BLK, BLK)
        qs = pl.multiple_of(n * BLK, BLK)
        q_pair = _split_pair(q_ref[pl.ds(qs, BLK), :])
        kb = k_ref[pl.ds(ks, 2 * BLK), :]
        vb = v_ref[pl.ds(ks, 2 * BLK), :]
        outs, lses = [], []
        for h in range(2):
            s = lax.dot_general(q_pair[h], kb, _NT, preferred_element_type=F32) + bias_ref[first, h]
            m = jnp.max(s, axis=-1, keepdims=True)
            p = jnp.exp(s - m)
            den = jnp.sum(p, axis=-1, keepdims=True)
            o = jnp.dot(p.astype(BF16), vb, preferred_element_type=F32)
            outs.append(o / den)
            lses.append(jnp.broadcast_to(m + jnp.log(den), (BLK, LANES)))
        o_ref[pl.ds(qs, BLK), :] = _merge_pair(outs[0], outs[1])
        lse_ref[pl.ds(qs, BLK), :] = _merge_pair(lses[0], lses[1])
        return carry

    lax.fori_loop(0, blocks_per_step, block, 0)


def _dilated(qa, ka, va, bias_tab, dil):
    b, t, _ = qa.shape
    tc = t // dil
    fold = lambda a: a.reshape(b, tc, dil * WIDTH_A)
    q_rows = min(tc, 8 * BLK)
    n_chunks = tc // q_rows
    col = lambda bi, r, p, c: r * PAIRS_A + p
    q_spec = pl.BlockSpec((None, q_rows, LANES), lambda bi, r, p, c: (bi, c, col(bi, r, p, c)))
    kv_spec = pl.BlockSpec((None, tc, LANES), lambda bi, r, p, c: (bi, 0, col(bi, r, p, c)))
    bias_spec = pl.BlockSpec((2, 2, BLK, 2 * BLK), lambda bi, r, p, c: (0, p, 0, 0))
    out_shape = jax.ShapeDtypeStruct((b, tc, dil * WIDTH_A), F32)
    o, lse = pl.pallas_call(
        functools.partial(_dilated_kernel, blocks_per_step=q_rows // BLK),
        grid=(b, dil, PAIRS_A, n_chunks),
        in_specs=[q_spec, kv_spec, kv_spec, bias_spec],
        out_specs=[q_spec, q_spec],
        out_shape=[out_shape, out_shape],
        compiler_params=_params(("parallel", "parallel", "parallel", "arbitrary")),
        name=f"dilated_d{dil}",
    )(fold(qa), fold(ka), fold(va), bias_tab)
    return o.reshape(b, t, WIDTH_A), lse.reshape(b, t, WIDTH_A)


def _softplus(z):
    return jnp.maximum(z, 0.0) + jnp.log1p(jnp.exp(-jnp.abs(z)))


def _sb_kernel(q_ref, k_ref, v_ref, tri_ref, o_ref, acc_ref, carry_ref):
    i = pl.program_id(2)
    bq = q_ref.shape[0]
    q0, q1 = _split_pair(q_ref[...])
    qs = jnp.concatenate([q0, q1], axis=0)
    tri = tri_ref[...]

    def step(j, diagonal):
        ks = pl.multiple_of(j * bq, bq)
        kb = k_ref[pl.ds(ks, bq), :]
        vb = v_ref[pl.ds(ks, bq), :]
        z = lax.dot_general(qs, kb, _NT, preferred_element_type=F32)
        t = _softplus(z)
        if diagonal:
            row = lax.broadcasted_iota(jnp.int32, z.shape, 0) & (bq - 1)
            colv = lax.broadcasted_iota(jnp.int32, z.shape, 1)
            causal = colv < row
            t = jnp.where(causal, t, 0.0)
        t_hi = t.astype(BF16)
        t_lo = (t - t_hi.astype(F32)).astype(BF16)
        after = (jnp.dot(t_hi, tri, preferred_element_type=F32)
                 + jnp.dot(t_lo, tri, preferred_element_type=F32))
        expo = (z - t) + after
        if not diagonal:
            carry = carry_ref[...]
            expo = expo + jnp.concatenate([carry] * (bq // LANES), axis=1)
        a = jnp.exp(expo)
        if diagonal:
            a = jnp.where(causal, a, 0.0)
        pv = jnp.dot(a.astype(BF16), vb, preferred_element_type=F32)
        block_sum = jnp.broadcast_to(after[:, 0:1] - t[:, 0:1], (2 * bq, LANES))
        if diagonal:
            acc_ref[...] = pv
            carry_ref[...] = block_sum
        else:
            acc_ref[...] += pv
            carry_ref[...] += block_sum

    step(i, True)

    def body(jj, c):
        step(i - 1 - jj, False)
        return c

    lax.fori_loop(0, i, body, 0)
    o_ref[...] = _merge_pair(acc_ref[0:bq, :], acc_ref[bq:2 * bq, :])


def _stick_breaking(qb, kb, vb):
    b, t, _ = qb.shape
    bq = SB_BLOCK
    r = np.arange(bq)
    tri = jnp.asarray(np.where(r[:, None] > r[None, :], -1.0, 0.0), dtype=BF16)
    q_spec = pl.BlockSpec((None, bq, LANES), lambda bi, p, i: (bi, i, p))
    kv_spec = pl.BlockSpec((None, t, LANES), lambda bi, p, i: (bi, 0, p))
    return pl.pallas_call(
        _sb_kernel,
        grid=(b, PAIRS_B, t // bq),
        in_specs=[q_spec, kv_spec, kv_spec, pl.BlockSpec((bq, bq), lambda bi, p, i: (0, 0))],
        out_specs=q_spec,
        out_shape=jax.ShapeDtypeStruct((b, t, WIDTH_B), F32),
        scratch_shapes=[pltpu.VMEM((2 * bq, LANES), F32), pltpu.VMEM((2 * bq, LANES), F32)],
        compiler_params=_params(("parallel", "parallel", "arbitrary")),
        name="stick_breaking",
    )(qb, kb, vb, tri)


def _out_kernel(o1_ref, l1_ref, o2_ref, l2_ref, o3_ref, l3_ref, yb_ref, gate_ref, h_ref,
                w_ref, b_ref, g_ref, beta_ref, hn_ref, hnb_ref):
    l1, l2, l3 = l1_ref[...], l2_ref[...], l3_ref[...]
    m = jnp.maximum(jnp.maximum(l1, l2), l3)
    w1, w2, w3 = jnp.exp(l1 - m), jnp.exp(l2 - m), jnp.exp(l3 - m)
    ya = (w1 * o1_ref[...] + w2 * o2_ref[...] + w3 * o3_ref[...]) / (w1 + w2 + w3)
    gate = gate_ref[...].astype(F32)
    silu = gate * (1.0 / (1.0 + jnp.exp(-gate)))
    ya = (ya * silu[:, :WIDTH_A]).astype(BF16)
    yb = (yb_ref[...] * silu[:, WIDTH_A:]).astype(BF16)
    out = (jnp.dot(ya, w_ref[0:WIDTH_A, :], preferred_element_type=F32)
           + jnp.dot(yb, w_ref[WIDTH_A:MIX_WIDTH, :], preferred_element_type=F32)
           + b_ref[...])
    y = _layer_norm_rows(DEEPNORM_ALPHA * h_ref[...] + out, g_ref[...], beta_ref[...])
    hn_ref[...] = y
    hnb_ref[...] = y.astype(BF16)


def _out_block(dil_outs, yb, gate, h, w_bf16, b_out, g, beta):
    n = h.shape[0]
    half = pl.BlockSpec((ROW_TILE, WIDTH_A), lambda i: (i, 0))
    full = pl.BlockSpec((ROW_TILE, D_MODEL), lambda i: (i, 0))
    vec = pl.BlockSpec((1, D_MODEL), lambda i: (0, 0))
    flat = [a.reshape(n, WIDTH_A) for pair in dil_outs for a in pair]
    return pl.pallas_call(
        _out_kernel,
        grid=(n // ROW_TILE,),
        in_specs=[half] * 6 + [half, full, full,
                               pl.BlockSpec((MIX_WIDTH, D_MODEL), lambda i: (0, 0)), vec, vec, vec],
        out_specs=[full, full],
        out_shape=[jax.ShapeDtypeStruct((n, D_MODEL), F32), jax.ShapeDtypeStruct((n, D_MODEL), BF16)],
        compiler_params=_params(("parallel",)),
        name="out_proj_norm",
    )(*flat, yb.reshape(n, WIDTH_B), gate, h, w_bf16,
      b_out.reshape(1, D_MODEL), g.reshape(1, D_MODEL), beta.reshape(1, D_MODEL))


def kernel(x, ln0_g, ln0_b, w_in, w_out, b_out, ln_g, ln_b, rel_bias):
    b, t, _ = x.shape
    n = b * t
    bias_tabs = _bias_tables(rel_bias)
    h, hb = _ln0(x.reshape(n, D_MODEL), ln0_g, ln0_b)
    for l in range(DEPTH):
        qa, ka, va, qb, kb, vb, gate = _project(hb, w_in[l].astype(BF16))
        seq = lambda a: a.reshape(b, t, a.shape[-1])
        dil_outs = [_dilated(seq(qa), seq(ka), seq(va), bias_tabs[c], dil)
                    for c, (_, dil) in enumerate(DILATED_CONFIGS)]
        yb = _stick_breaking(seq(qb), seq(kb), seq(vb))
        h, hb = _out_block(dil_outs, yb, gate, h, w_out[l].astype(BF16), b_out[l], ln_g[l], ln_b[l])
    return h.reshape(b, t, D_MODEL)
```

```python
import functools
import math

import numpy as np
import jax
import jax.numpy as jnp
from jax import lax
from jax.experimental import pallas as pl
from jax.experimental.pallas import tpu as pltpu

D_MODEL = 1024
DEPTH = 2
HEAD_DIM = 64
N_HEADS_A = 8
N_HEADS_B = 8
WIDTH_A = N_HEADS_A * HEAD_DIM
WIDTH_B = N_HEADS_B * HEAD_DIM
MIX_WIDTH = WIDTH_A + WIDTH_B
IN_COLS = 3 * WIDTH_A + 3 * WIDTH_B + MIX_WIDTH
DILATED_CONFIGS = ((128, 1), (512, 4), (2048, 16))
BLK = 128
N_BUCKETS = 32
MAX_DISTANCE = 2048
LN_EPS = 1e-5
DEEPNORM_ALPHA = (2.0 * DEPTH) ** 0.25
QK_SCALE = 1.0 / math.sqrt(HEAD_DIM)

LANES = 128
PAIRS_A = WIDTH_A // LANES
PAIRS_B = WIDTH_B // LANES
NEG_BIG = -1e30

ROW_TILE = 512
SB_BLOCK = 256
SB_UNROLL = 4
VMEM_LIMIT = 48 * 1024 * 1024

F32 = jnp.float32
BF16 = jnp.bfloat16


def _params(semantics):
    return pltpu.CompilerParams(dimension_semantics=semantics, vmem_limit_bytes=VMEM_LIMIT)


def _layer_norm_rows(x, g, b):
    mu = jnp.mean(x, axis=-1, keepdims=True)
    xc = x - mu
    var = jnp.mean(xc * xc, axis=-1, keepdims=True)
    return xc * lax.rsqrt(var + LN_EPS) * g + b


def _ln0_kernel(x_ref, g_ref, b_ref, h_ref, hb_ref):
    y = _layer_norm_rows(x_ref[...], g_ref[...], b_ref[...])
    h_ref[...] = y
    hb_ref[...] = y.astype(BF16)


def _ln0(x2, g, b):
    n = x2.shape[0]
    row = pl.BlockSpec((ROW_TILE, D_MODEL), lambda i: (i, 0))
    vec = pl.BlockSpec((1, D_MODEL), lambda i: (0, 0))
    return pl.pallas_call(
        _ln0_kernel,
        grid=(n // ROW_TILE,),
        in_specs=[row, vec, vec],
        out_specs=[row, row],
        out_shape=[jax.ShapeDtypeStruct((n, D_MODEL), F32), jax.ShapeDtypeStruct((n, D_MODEL), BF16)],
        compiler_params=_params(("parallel",)),
        name="ln0",
    )(x2, g.reshape(1, D_MODEL), b.reshape(1, D_MODEL))


_PROJ_OUTS = (
    (0, WIDTH_A, QK_SCALE), (WIDTH_A, WIDTH_A, 1.0), (2 * WIDTH_A, WIDTH_A, 1.0),
    (3 * WIDTH_A, WIDTH_B, QK_SCALE), (3 * WIDTH_A + WIDTH_B, WIDTH_B, 1.0),
    (3 * WIDTH_A + 2 * WIDTH_B, WIDTH_B, 1.0), (3 * WIDTH_A + 3 * WIDTH_B, MIX_WIDTH, 1.0),
)


def _proj_kernel(h_ref, w_ref, *out_refs):
    h = h_ref[...]
    for (c0, width, scale), o_ref in zip(_PROJ_OUTS, out_refs):
        acc = jnp.dot(h, w_ref[:, c0:c0 + width], preferred_element_type=F32)
        if scale != 1.0:
            acc = acc * scale
        o_ref[...] = acc.astype(BF16)


def _project(hb, w_bf16):
    n = hb.shape[0]
    return pl.pallas_call(
        _proj_kernel,
        grid=(n // ROW_TILE,),
        in_specs=[pl.BlockSpec((ROW_TILE, D_MODEL), lambda i: (i, 0)),
                  pl.BlockSpec((D_MODEL, IN_COLS), lambda i: (0, 0))],
        out_specs=[pl.BlockSpec((ROW_TILE, w), lambda i: (i, 0)) for _, w, _ in _PROJ_OUTS],
        out_shape=[jax.ShapeDtypeStruct((n, w), BF16) for _, w, _ in _PROJ_OUTS],
        compiler_params=_params(("parallel",)),
        name="in_proj",
    )(hb, w_bf16)


def _split_pair(x_bf16):
    lane = lax.broadcasted_iota(jnp.int32, x_bf16.shape, 1)
    x = x_bf16.astype(F32)
    lo = jnp.where(lane < HEAD_DIM, x, 0.0).astype(BF16)
    hi = jnp.where(lane >= HEAD_DIM, x, 0.0).astype(BF16)
    return lo, hi


def _merge_pair(x0, x1):
    lane = lax.broadcasted_iota(jnp.int32, x0.shape, 1)
    return jnp.where(lane < HEAD_DIM, x0, x1)


_NT = (((1,), (1,)), ((), ()))


def _t5_bucket(dist):
    max_exact = N_BUCKETS // 2
    n = np.maximum(dist, 1).astype(np.float32)
    large = max_exact + (np.log(n / max_exact) / math.log(MAX_DISTANCE / max_exact)
                         * (N_BUCKETS - max_exact)).astype(np.int32)
    large = np.minimum(large, N_BUCKETS - 1)
    return np.where(dist < max_exact, dist, large).astype(np.int32)


def _bias_tables(rel_bias):
    i = np.arange(BLK)[:, None]
    j = np.arange(2 * BLK)[None, :]
    tabs = []
    for window, dil in DILATED_CONFIGS:
        wc = window // dil
        delta = BLK + i - j
        band = (delta >= 0) & (delta <= wc)
        bucket = _t5_bucket(np.clip(delta, 0, wc) * dil)
        bias = jnp.transpose(rel_bias[bucket], (2, 0, 1)).astype(F32)
        full = jnp.where(band[None], bias, NEG_BIG)
        first = jnp.concatenate([full[:, :, BLK:], jnp.full_like(full[:, :, BLK:], NEG_BIG)], axis=2)
        tabs.append(jnp.stack([full, first], axis=0))
    return jnp.stack(tabs, axis=0)


def _dilated_kernel(q_ref, k_ref, v_ref, bias_ref, o_ref, lse_ref, *, blocks_per_step):
    c = pl.program_id(3)

    def block(n, carry):
        g = c * blocks_per_step + n
        first = (g == 0).astype(jnp.int32)
        ks = pl.multiple_of(jnp.maximum(g - 1, 0) * BLK, BLK)
        qs = pl.multiple_of(n * BLK, BLK)
        q_pair = _split_pair(q_ref[pl.ds(qs, BLK), :])
        kb = k_ref[pl.ds(ks, 2 * BLK), :]
        vb = v_ref[pl.ds(ks, 2 * BLK), :]
        outs, lses = [], []
        for h in range(2):
            s = lax.dot_general(q_pair[h], kb, _NT, preferred_element_type=F32) + bias_ref[first, h]
            m = jnp.max(s, axis=-1, keepdims=True)
            p = jnp.exp(s - m)
            den = jnp.sum(p, axis=-1, keepdims=True)
            o = jnp.dot(p.astype(BF16), vb, preferred_element_type=F32)
            outs.append(o / den)
            lses.append(jnp.broadcast_to(m + jnp.log(den), (BLK, LANES)))
        o_ref[pl.ds(qs, BLK), :] = _merge_pair(outs[0], outs[1])
        lse_ref[pl.ds(qs, BLK), :] = _merge_pair(lses[0], lses[1])
        return carry

    lax.fori_loop(0, blocks_per_step, block, 0)


def _dilated(qa, ka, va, bias_tab, dil):
    b, t, _ = qa.shape
    tc = t // dil
    fold = lambda a: a.reshape(b, tc, dil * WIDTH_A)
    q_rows = min(tc, 8 * BLK)
    n_chunks = tc // q_rows
    col = lambda bi, r, p, c: r * PAIRS_A + p
    q_spec = pl.BlockSpec((None, q_rows, LANES), lambda bi, r, p, c: (bi, c, col(bi, r, p, c)))
    kv_spec = pl.BlockSpec((None, tc, LANES), lambda bi, r, p, c: (bi, 0, col(bi, r, p, c)))
    bias_spec = pl.BlockSpec((2, 2, BLK, 2 * BLK), lambda bi, r, p, c: (0, p, 0, 0))
    out_shape = jax.ShapeDtypeStruct((b, tc, dil * WIDTH_A), F32)
    o, lse = pl.pallas_call(
        functools.partial(_dilated_kernel, blocks_per_step=q_rows // BLK),
        grid=(b, dil, PAIRS_A, n_chunks),
        in_specs=[q_spec, kv_spec, kv_spec, bias_spec],
        out_specs=[q_spec, q_spec],
        out_shape=[out_shape, out_shape],
        compiler_params=_params(("parallel", "parallel", "parallel", "arbitrary")),
        name=f"dilated_d{dil}",
    )(fold(qa), fold(ka), fold(va), bias_tab)
    return o.reshape(b, t, WIDTH_A), lse.reshape(b, t, WIDTH_A)


LOG2E = math.log2(math.e)


def _softplus(z):
    return jnp.maximum(z, 0.0) + jnp.log(1.0 + jnp.exp2(jnp.abs(z) * (-LOG2E)))


def _sb_kernel(q_ref, k_ref, v_ref, tri_ref, o_ref, acc_ref, carry_ref):
    i = pl.program_id(2)
    bq = q_ref.shape[0]
    q0, q1 = _split_pair(q_ref[...])
    qs = jnp.concatenate([q0, q1], axis=0)
    tri = tri_ref[...]

    def step(j, carry):
        diagonal = carry is None
        ks = pl.multiple_of(j * bq, bq)
        kb = k_ref[pl.ds(ks, bq), :]
        vb = v_ref[pl.ds(ks, bq), :]
        z = lax.dot_general(qs, kb, _NT, preferred_element_type=F32)
        t = _softplus(z)
        if diagonal:
            row = lax.broadcasted_iota(jnp.int32, z.shape, 0) & (bq - 1)
            colv = lax.broadcasted_iota(jnp.int32, z.shape, 1)
            causal = colv < row
            t = jnp.where(causal, t, 0.0)
        after = jnp.dot(t.astype(BF16), tri, preferred_element_type=F32)
        expo = (z - t) + after
        if not diagonal:
            expo = expo + jnp.concatenate([carry] * (bq // LANES), axis=1)
        a = jnp.exp2(expo * LOG2E)
        if diagonal:
            a = jnp.where(causal, a, 0.0)
        pv = jnp.dot(a.astype(BF16), vb, preferred_element_type=F32)
        block_sum = jnp.broadcast_to(after[:, 0:1] - t[:, 0:1], (2 * bq, LANES))
        return pv, block_sum

    def run(first_block, count, acc, carry):
        for u in range(count):
            pv, block_sum = step(first_block - u, carry)
            acc, carry = acc + pv, carry + block_sum
        return acc, carry

    acc_ref[...], carry_ref[...] = step(i, None)

    def group(g, c):
        acc_ref[...], carry_ref[...] = run(i - 1 - g * SB_UNROLL, SB_UNROLL, acc_ref[...], carry_ref[...])
        return c

    n_groups = i // SB_UNROLL
    lax.fori_loop(0, n_groups, group, 0)
    rem = i - n_groups * SB_UNROLL
    for u in range(SB_UNROLL - 1):
        @pl.when(rem > u)
        def _():
            acc_ref[...], carry_ref[...] = run(rem - 1 - u, 1, acc_ref[...], carry_ref[...])

    o_ref[...] = _merge_pair(acc_ref[0:bq, :], acc_ref[bq:2 * bq, :])


def _stick_breaking(qb, kb, vb):
    b, t, _ = qb.shape
    bq = SB_BLOCK
    r = np.arange(bq)
    tri = jnp.asarray(np.where(r[:, None] > r[None, :], -1.0, 0.0), dtype=BF16)
    q_spec = pl.BlockSpec((None, bq, LANES), lambda bi, p, i: (bi, i, p))
    kv_spec = pl.BlockSpec((None, t, LANES), lambda bi, p, i: (bi, 0, p))
    return pl.pallas_call(
        _sb_kernel,
        grid=(b, PAIRS_B, t // bq),
        in_specs=[q_spec, kv_spec, kv_spec, pl.BlockSpec((bq, bq), lambda bi, p, i: (0, 0))],
        out_specs=q_spec,
        out_shape=jax.ShapeDtypeStruct((b, t, WIDTH_B), F32),
        scratch_shapes=[pltpu.VMEM((2 * bq, LANES), F32), pltpu.VMEM((2 * bq, LANES), F32)],
        compiler_params=_params(("parallel", "parallel", "arbitrary")),
        name="stick_breaking",
    )(qb, kb, vb, tri)


def _out_kernel(o1_ref, l1_ref, o2_ref, l2_ref, o3_ref, l3_ref, yb_ref, gate_ref, h_ref,
                w_ref, b_ref, g_ref, beta_ref, hn_ref, hnb_ref):
    l1, l2, l3 = l1_ref[...], l2_ref[...], l3_ref[...]
    m = jnp.maximum(jnp.maximum(l1, l2), l3)
    w1, w2, w3 = jnp.exp(l1 - m), jnp.exp(l2 - m), jnp.exp(l3 - m)
    ya = (w1 * o1_ref[...] + w2 * o2_ref[...] + w3 * o3_ref[...]) / (w1 + w2 + w3)
    gate = gate_ref[...].astype(F32)
    silu = gate * (1.0 / (1.0 + jnp.exp(-gate)))
    ya = (ya * silu[:, :WIDTH_A]).astype(BF16)
    yb = (yb_ref[...] * silu[:, WIDTH_A:]).astype(BF16)
    out = (jnp.dot(ya, w_ref[0:WIDTH_A, :], preferred_element_type=F32)
           + jnp.dot(yb, w_ref[WIDTH_A:MIX_WIDTH, :], preferred_element_type=F32)
           + b_ref[...])
    y = _layer_norm_rows(DEEPNORM_ALPHA * h_ref[...] + out, g_ref[...], beta_ref[...])
    hn_ref[...] = y
    hnb_ref[...] = y.astype(BF16)


def _out_block(dil_outs, yb, gate, h, w_bf16, b_out, g, beta):
    n = h.shape[0]
    half = pl.BlockSpec((ROW_TILE, WIDTH_A), lambda i: (i, 0))
    full = pl.BlockSpec((ROW_TILE, D_MODEL), lambda i: (i, 0))
    vec = pl.BlockSpec((1, D_MODEL), lambda i: (0, 0))
    flat = [a.reshape(n, WIDTH_A) for pair in dil_outs for a in pair]
    return pl.pallas_call(
        _out_kernel,
        grid=(n // ROW_TILE,),
        in_specs=[half] * 6 + [half, full, full,
                               pl.BlockSpec((MIX_WIDTH, D_MODEL), lambda i: (0, 0)), vec, vec, vec],
        out_specs=[full, full],
        out_shape=[jax.ShapeDtypeStruct((n, D_MODEL), F32), jax.ShapeDtypeStruct((n, D_MODEL), BF16)],
        compiler_params=_params(("parallel",)),
        name="out_proj_norm",
    )(*flat, yb.reshape(n, WIDTH_B), gate, h, w_bf16,
      b_out.reshape(1, D_MODEL), g.reshape(1, D_MODEL), beta.reshape(1, D_MODEL))


def kernel(x, ln0_g, ln0_b, w_in, w_out, b_out, ln_g, ln_b, rel_bias):
    b, t, _ = x.shape
    n = b * t
    bias_tabs = _bias_tables(rel_bias)
    h, hb = _ln0(x.reshape(n, D_MODEL), ln0_g, ln0_b)
    for l in range(DEPTH):
        qa, ka, va, qb, kb, vb, gate = _project(hb, w_in[l].astype(BF16))
        seq = lambda a: a.reshape(b, t, a.shape[-1])
        dil_outs = [_dilated(seq(qa), seq(ka), seq(va), bias_tabs[c], dil)
                    for c, (_, dil) in enumerate(DILATED_CONFIGS)]
        yb = _stick_breaking(seq(qb), seq(kb), seq(vb))
        h, hb = _out_block(dil_outs, yb, gate, h, w_out[l].astype(BF16), b_out[l], ln_g[l], ln_b[l])
    return h.reshape(b, t, D_MODEL)
```

```python
import math

import numpy as np
import jax
import jax.numpy as jnp
from jax import lax
from jax.experimental import pallas as pl
from jax.experimental.pallas import tpu as pltpu

D_MODEL = 1024
DEPTH = 2
HEAD_DIM = 64
N_HEADS_A = 8
N_HEADS_B = 8
WIDTH_A = N_HEADS_A * HEAD_DIM
WIDTH_B = N_HEADS_B * HEAD_DIM
MIX_WIDTH = WIDTH_A + WIDTH_B
IN_COLS = 3 * WIDTH_A + 3 * WIDTH_B + MIX_WIDTH
DILATED_CONFIGS = ((128, 1), (512, 4), (2048, 16))
BLK = 128
N_BUCKETS = 32
MAX_DISTANCE = 2048
LN_EPS = 1e-5
DEEPNORM_ALPHA = (2.0 * DEPTH) ** 0.25
QK_SCALE = 1.0 / math.sqrt(HEAD_DIM)
LOG2E = math.log2(math.e)

LANES = 128
PAIRS_A = WIDTH_A // LANES
PAIRS_B = WIDTH_B // LANES
NEG_BIG = -1e30

ROW_TILE = 512
SB_BLOCK = 256
SB_UNROLL = 4
DIL_CHUNK = BLK * max(d for _, d in DILATED_CONFIGS)
DIL_BLOCKS = DIL_CHUNK // BLK
DIL_GROUP = 8
VMEM_LIMIT = 48 * 1024 * 1024

F32 = jnp.float32
BF16 = jnp.bfloat16


def _params(semantics):
    return pltpu.CompilerParams(dimension_semantics=semantics, vmem_limit_bytes=VMEM_LIMIT)


def _layer_norm_rows(x, g, b):
    mu = jnp.mean(x, axis=-1, keepdims=True)
    xc = x - mu
    var = jnp.mean(xc * xc, axis=-1, keepdims=True)
    return xc * lax.rsqrt(var + LN_EPS) * g + b


def _ln0_kernel(x_ref, g_ref, b_ref, h_ref, hb_ref):
    y = _layer_norm_rows(x_ref[...], g_ref[...], b_ref[...])
    h_ref[...] = y
    hb_ref[...] = y.astype(BF16)


def _ln0(x2, g, b):
    n = x2.shape[0]
    row = pl.BlockSpec((ROW_TILE, D_MODEL), lambda i: (i, 0))
    vec = pl.BlockSpec((1, D_MODEL), lambda i: (0, 0))
    return pl.pallas_call(
        _ln0_kernel,
        grid=(n // ROW_TILE,),
        in_specs=[row, vec, vec],
        out_specs=[row, row],
        out_shape=[jax.ShapeDtypeStruct((n, D_MODEL), F32), jax.ShapeDtypeStruct((n, D_MODEL), BF16)],
        compiler_params=_params(("parallel",)),
        name="ln0",
    )(x2, g.reshape(1, D_MODEL), b.reshape(1, D_MODEL))


_PROJ_OUTS = (
    (0, WIDTH_A, QK_SCALE, F32), (WIDTH_A, WIDTH_A, 1.0, F32), (2 * WIDTH_A, WIDTH_A, 1.0, F32),
    (3 * WIDTH_A, WIDTH_B, QK_SCALE, BF16), (3 * WIDTH_A + WIDTH_B, WIDTH_B, 1.0, BF16),
    (3 * WIDTH_A + 2 * WIDTH_B, WIDTH_B, 1.0, BF16), (3 * WIDTH_A + 3 * WIDTH_B, MIX_WIDTH, 1.0, BF16),
)


def _proj_kernel(h_ref, w_ref, *out_refs):
    h = h_ref[...]
    for (c0, width, scale, _), o_ref in zip(_PROJ_OUTS, out_refs):
        acc = jnp.dot(h, w_ref[:, c0:c0 + width], preferred_element_type=F32)
        if scale != 1.0:
            acc = acc * scale
        o_ref[...] = acc.astype(o_ref.dtype)


def _project(hb, w_bf16):
    n = hb.shape[0]
    return pl.pallas_call(
        _proj_kernel,
        grid=(n // ROW_TILE,),
        in_specs=[pl.BlockSpec((ROW_TILE, D_MODEL), lambda i: (i, 0)),
                  pl.BlockSpec((D_MODEL, IN_COLS), lambda i: (0, 0))],
        out_specs=[pl.BlockSpec((ROW_TILE, w), lambda i: (i, 0)) for _, w, _, _ in _PROJ_OUTS],
        out_shape=[jax.ShapeDtypeStruct((n, w), dt) for _, w, _, dt in _PROJ_OUTS],
        compiler_params=_params(("parallel",)),
        name="in_proj",
    )(hb, w_bf16)


def _split_pair(x):
    lane = lax.broadcasted_iota(jnp.int32, x.shape, 1)
    x = x.astype(F32)
    lo = jnp.where(lane < HEAD_DIM, x, 0.0).astype(BF16)
    hi = jnp.where(lane >= HEAD_DIM, x, 0.0).astype(BF16)
    return lo, hi


def _merge_pair(x0, x1):
    lane = lax.broadcasted_iota(jnp.int32, x0.shape, 1)
    return jnp.where(lane < HEAD_DIM, x0, x1)


_NT = (((1,), (1,)), ((), ()))


def _t5_bucket(dist):
    max_exact = N_BUCKETS // 2
    n = np.maximum(dist, 1).astype(np.float32)
    large = max_exact + (np.log(n / max_exact) / math.log(MAX_DISTANCE / max_exact)
                         * (N_BUCKETS - max_exact)).astype(np.int32)
    large = np.minimum(large, N_BUCKETS - 1)
    return np.where(dist < max_exact, dist, large).astype(np.int32)


def _bucket_segments(window, dil):
    buckets = _t5_bucket(np.arange(window // dil + 1) * dil)
    return [(d, int(bk)) for d, bk in enumerate(buckets) if d == 0 or bk != buckets[d - 1]]


def _bias_kernel(rel_ref, out_ref):
    i = lax.broadcasted_iota(jnp.int32, (BLK, 2 * BLK), 0)
    j = lax.broadcasted_iota(jnp.int32, (BLK, 2 * BLK), 1)
    for cfg, (window, dil) in enumerate(DILATED_CONFIGS):
        segments = _bucket_segments(window, dil)
        for variant in range(2):
            delta = (BLK + i - j) if variant == 0 else (i - j)
            band = (delta >= 0) & (delta <= window // dil)
            steps = [delta >= start for start, _ in segments[1:]]
            for h in range(N_HEADS_A):
                val = jnp.full((BLK, 2 * BLK), rel_ref[segments[0][1], h], F32)
                for step, (_, bucket) in zip(steps, segments[1:]):
                    val = jnp.where(step, rel_ref[bucket, h], val)
                out_ref[cfg, variant, h] = jnp.where(band, val, NEG_BIG)


def _bias_tables(rel_bias):
    shape = (len(DILATED_CONFIGS), 2, N_HEADS_A, BLK, 2 * BLK)
    return pl.pallas_call(
        _bias_kernel,
        in_specs=[pl.BlockSpec(memory_space=pltpu.SMEM)],
        out_shape=jax.ShapeDtypeStruct(shape, F32),
        compiler_params=pltpu.CompilerParams(vmem_limit_bytes=VMEM_LIMIT),
        name="bias_tables",
    )(rel_bias.astype(F32))


def _strided_rows(start, size, stride):
    return pl.ds(start, size) if stride == 1 else pl.ds(start, size, stride=stride)


def _dilated_kernel(q_ref, k_ref, v_ref, bias_ref, o_ref, m_ref, den_ref, num_ref):
    c = pl.program_id(2)
    last_cfg = len(DILATED_CONFIGS) - 1

    def load_block(cfg, dil, e):
        per_res = DIL_BLOCKS // dil
        r = e // per_res if isinstance(e, int) else lax.shift_right_logical(e, int(math.log2(per_res)))
        n = e - r * per_res
        g = c * per_res + n
        rows = _strided_rows(n * (BLK * dil) + r, BLK, dil)
        k_rows = _strided_rows(jnp.maximum(g - 1, 0) * (BLK * dil) + r, 2 * BLK, dil)
        blk = dict(rows=rows, first=jnp.where(g == 0, 1, 0), q=q_ref[rows, :],
                   k=k_ref[k_rows, :].astype(BF16), v=v_ref[k_rows, :].astype(BF16))
        if cfg > 0:
            blk.update(m=m_ref[rows, :], den=den_ref[rows, :], num=num_ref[rows, :])
        return blk

    def attend(cfg, blk):
        q_pair = _split_pair(blk["q"])
        ms, dens, nums = [], [], []
        for h in range(2):
            s = lax.dot_general(q_pair[h], blk["k"], _NT, preferred_element_type=F32)
            s = s + bias_ref[cfg, blk["first"], h]
            m = jnp.max(s, axis=-1, keepdims=True)
            p = jnp.exp(s - m)
            ms.append(jnp.broadcast_to(m, (BLK, LANES)))
            dens.append(jnp.broadcast_to(jnp.sum(p, axis=-1, keepdims=True), (BLK, LANES)))
            nums.append(jnp.dot(p.astype(BF16), blk["v"], preferred_element_type=F32))
        m, den, num = _merge_pair(*ms), _merge_pair(*dens), _merge_pair(*nums)
        if cfg > 0:
            m_new = jnp.maximum(blk["m"], m)
            w_old, w_new = jnp.exp(blk["m"] - m_new), jnp.exp(m - m_new)
            den = blk["den"] * w_old + den * w_new
            num = blk["num"] * w_old + num * w_new
            m = m_new
        return m, den, num

    for cfg, (_, dil) in enumerate(DILATED_CONFIGS):
        def group(it, carry, cfg=cfg, dil=dil):
            blocks = [load_block(cfg, dil, it * DIL_GROUP + u) for u in range(DIL_GROUP)]
            results = [attend(cfg, blk) for blk in blocks]
            for blk, (m, den, num) in zip(blocks, results):
                if cfg == last_cfg:
                    num_ref[blk["rows"], :] = num / den
                else:
                    m_ref[blk["rows"], :] = m
                    den_ref[blk["rows"], :] = den
                    num_ref[blk["rows"], :] = num
            return carry

        lax.fori_loop(0, DIL_BLOCKS // DIL_GROUP, group, 0)

    o_ref[...] = num_ref[...].astype(o_ref.dtype)


def _dilated(qa, ka, va, bias_tabs):
    b, t, _ = qa.shape
    assert t % DIL_CHUNK == 0
    q_spec = pl.BlockSpec((None, DIL_CHUNK, LANES), lambda bi, p, c: (bi, c, p))
    kv_spec = pl.BlockSpec((None, t, LANES), lambda bi, p, c: (bi, 0, p))
    bias_spec = pl.BlockSpec((len(DILATED_CONFIGS), 2, 2, BLK, 2 * BLK), lambda bi, p, c: (0, 0, p, 0, 0))
    return pl.pallas_call(
        _dilated_kernel,
        grid=(b, PAIRS_A, t // DIL_CHUNK),
        in_specs=[q_spec, kv_spec, kv_spec, bias_spec],
        out_specs=q_spec,
        out_shape=jax.ShapeDtypeStruct((b, t, WIDTH_A), BF16),
        scratch_shapes=[pltpu.VMEM((DIL_CHUNK, LANES), F32)] * 3,
        compiler_params=_params(("parallel", "parallel", "arbitrary")),
        name="dilated",
    )(qa, ka, va, bias_tabs)


def _softplus(z):
    return jnp.maximum(z, 0.0) + jnp.log(1.0 + jnp.exp2(jnp.abs(z) * (-LOG2E)))


def _sb_kernel(q_ref, k_ref, v_ref, tri_ref, o_ref, acc_ref, carry_ref):
    i = pl.program_id(2)
    bq = q_ref.shape[0]
    q0, q1 = _split_pair(q_ref[...])
    qs = jnp.concatenate([q0, q1], axis=0)
    tri = tri_ref[...]

    def step(j, carry):
        diagonal = carry is None
        ks = pl.multiple_of(j * bq, bq)
        kb = k_ref[pl.ds(ks, bq), :]
        vb = v_ref[pl.ds(ks, bq), :]
        z = lax.dot_general(qs, kb, _NT, preferred_element_type=F32)
        t = _softplus(z)
        if diagonal:
            row = lax.broadcasted_iota(jnp.int32, z.shape, 0) & (bq - 1)
            colv = lax.broadcasted_iota(jnp.int32, z.shape, 1)
            causal = colv < row
            t = jnp.where(causal, t, 0.0)
        after = jnp.dot(t.astype(BF16), tri, preferred_element_type=F32)
        expo = (z - t) + after
        if not diagonal:
            expo = expo + jnp.concatenate([carry] * (bq // LANES), axis=1)
        a = jnp.exp2(expo * LOG2E)
        if diagonal:
            a = jnp.where(causal, a, 0.0)
        pv = jnp.dot(a.astype(BF16), vb, preferred_element_type=F32)
        block_sum = jnp.broadcast_to(after[:, 0:1] - t[:, 0:1], (2 * bq, LANES))
        return pv, block_sum

    def run(first_block, count, acc, carry):
        for u in range(count):
            pv, block_sum = step(first_block - u, carry)
            acc, carry = acc + pv, carry + block_sum
        return acc, carry

    acc_ref[...], carry_ref[...] = step(i, None)

    def group(g, c):
        acc_ref[...], carry_ref[...] = run(i - 1 - g * SB_UNROLL, SB_UNROLL, acc_ref[...], carry_ref[...])
        return c

    n_groups = i // SB_UNROLL
    lax.fori_loop(0, n_groups, group, 0)
    rem = i - n_groups * SB_UNROLL
    for u in range(SB_UNROLL - 1):
        @pl.when(rem > u)
        def _():
            acc_ref[...], carry_ref[...] = run(rem - 1 - u, 1, acc_ref[...], carry_ref[...])

    o_ref[...] = _merge_pair(acc_ref[0:bq, :], acc_ref[bq:2 * bq, :]).astype(o_ref.dtype)


def _stick_breaking(qb, kb, vb):
    b, t, _ = qb.shape
    bq = SB_BLOCK
    assert t % bq == 0
    r = np.arange(bq)
    tri = jnp.asarray(np.where(r[:, None] > r[None, :], -1.0, 0.0), dtype=BF16)
    q_spec = pl.BlockSpec((None, bq, LANES), lambda bi, p, i: (bi, i, p))
    kv_spec = pl.BlockSpec((None, t, LANES), lambda bi, p, i: (bi, 0, p))
    return pl.pallas_call(
        _sb_kernel,
        grid=(b, PAIRS_B, t // bq),
        in_specs=[q_spec, kv_spec, kv_spec, pl.BlockSpec((bq, bq), lambda bi, p, i: (0, 0))],
        out_specs=q_spec,
        out_shape=jax.ShapeDtypeStruct((b, t, WIDTH_B), BF16),
        scratch_shapes=[pltpu.VMEM((2 * bq, LANES), F32), pltpu.VMEM((2 * bq, LANES), F32)],
        compiler_params=_params(("parallel", "parallel", "arbitrary")),
        name="stick_breaking",
    )(qb, kb, vb, tri)


def _out_kernel(ya_ref, yb_ref, gate_ref, h_ref, w_ref, b_ref, g_ref, beta_ref, hn_ref, hnb_ref):
    gate = gate_ref[...].astype(F32)
    silu = gate * (1.0 / (1.0 + jnp.exp(-gate)))
    ya = (ya_ref[...].astype(F32) * silu[:, :WIDTH_A]).astype(BF16)
    yb = (yb_ref[...].astype(F32) * silu[:, WIDTH_A:]).astype(BF16)
    out = (jnp.dot(ya, w_ref[0:WIDTH_A, :], preferred_element_type=F32)
           + jnp.dot(yb, w_ref[WIDTH_A:MIX_WIDTH, :], preferred_element_type=F32)
           + b_ref[...])
    y = _layer_norm_rows(DEEPNORM_ALPHA * h_ref[...] + out, g_ref[...], beta_ref[...])
    hn_ref[...] = y
    hnb_ref[...] = y.astype(BF16)


def _out_block(ya, yb, gate, h, w_bf16, b_out, g, beta):
    n = h.shape[0]
    half = pl.BlockSpec((ROW_TILE, WIDTH_A), lambda i: (i, 0))
    full = pl.BlockSpec((ROW_TILE, D_MODEL), lambda i: (i, 0))
    vec = pl.BlockSpec((1, D_MODEL), lambda i: (0, 0))
    return pl.pallas_call(
        _out_kernel,
        grid=(n // ROW_TILE,),
        in_specs=[half, half, full, full, pl.BlockSpec((MIX_WIDTH, D_MODEL), lambda i: (0, 0)), vec, vec, vec],
        out_specs=[full, full],
        out_shape=[jax.ShapeDtypeStruct((n, D_MODEL), F32), jax.ShapeDtypeStruct((n, D_MODEL), BF16)],
        compiler_params=_params(("parallel",)),
        name="out_proj_norm",
    )(ya.reshape(n, WIDTH_A), yb.reshape(n, WIDTH_B), gate, h, w_bf16,
      b_out.reshape(1, D_MODEL), g.reshape(1, D_MODEL), beta.reshape(1, D_MODEL))


def kernel(x, ln0_g, ln0_b, w_in, w_out, b_out, ln_g, ln_b, rel_bias):
    b, t, _ = x.shape
    n = b * t
    bias_tabs = _bias_tables(rel_bias)
    h, hb = _ln0(x.reshape(n, D_MODEL), ln0_g, ln0_b)
    for l in range(DEPTH):
        qa, ka, va, qb, kb, vb, gate = _project(hb, w_in[l].astype(BF16))
        seq = lambda a: a.reshape(b, t, a.shape[-1])
        ya = _dilated(seq(qa), seq(ka), seq(va), bias_tabs)
        yb = _stick_breaking(seq(qb), seq(kb), seq(vb))
        h, hb = _out_block(ya, yb, gate, h, w_out[l].astype(BF16), b_out[l], ln_g[l], ln_b[l])
    return h.reshape(b, t, D_MODEL)
```

```python
import math

import numpy as np
import jax
import jax.numpy as jnp
from jax import lax
from jax.experimental import pallas as pl
from jax.experimental.pallas import tpu as pltpu

D_MODEL = 1024
DEPTH = 2
HEAD_DIM = 64
N_HEADS_A = 8
N_HEADS_B = 8
WIDTH_A = N_HEADS_A * HEAD_DIM
WIDTH_B = N_HEADS_B * HEAD_DIM
MIX_WIDTH = WIDTH_A + WIDTH_B
IN_COLS = 3 * WIDTH_A + 3 * WIDTH_B + MIX_WIDTH
DILATED_CONFIGS = ((128, 1), (512, 4), (2048, 16))
BLK = 128
N_BUCKETS = 32
MAX_DISTANCE = 2048
LN_EPS = 1e-5
DEEPNORM_ALPHA = (2.0 * DEPTH) ** 0.25
QK_SCALE = 1.0 / math.sqrt(HEAD_DIM)
LOG2E = math.log2(math.e)

LANES = 128
PAIRS_A = WIDTH_A // LANES
PAIRS_B = WIDTH_B // LANES
NEG_BIG = -1e30

ROW_TILE = 512
SB_BLOCK = 256
SB_DEAD = -110.0
DIL_CHUNK = BLK * max(d for _, d in DILATED_CONFIGS)
DIL_BLOCKS = DIL_CHUNK // BLK
DIL_GROUP = 8
VMEM_LIMIT = 48 * 1024 * 1024

F32 = jnp.float32
BF16 = jnp.bfloat16


def _params(semantics):
    return pltpu.CompilerParams(dimension_semantics=semantics, vmem_limit_bytes=VMEM_LIMIT)


def _layer_norm_rows(x, g, b):
    mu = jnp.mean(x, axis=-1, keepdims=True)
    xc = x - mu
    var = jnp.mean(xc * xc, axis=-1, keepdims=True)
    return xc * lax.rsqrt(var + LN_EPS) * g + b


def _ln0_kernel(x_ref, g_ref, b_ref, h_ref, hb_ref):
    y = _layer_norm_rows(x_ref[...], g_ref[...], b_ref[...])
    h_ref[...] = y
    hb_ref[...] = y.astype(BF16)


def _ln0(x2, g, b):
    n = x2.shape[0]
    row = pl.BlockSpec((ROW_TILE, D_MODEL), lambda i: (i, 0))
    vec = pl.BlockSpec((1, D_MODEL), lambda i: (0, 0))
    return pl.pallas_call(
        _ln0_kernel,
        grid=(n // ROW_TILE,),
        in_specs=[row, vec, vec],
        out_specs=[row, row],
        out_shape=[jax.ShapeDtypeStruct((n, D_MODEL), F32), jax.ShapeDtypeStruct((n, D_MODEL), BF16)],
        compiler_params=_params(("parallel",)),
        name="ln0",
    )(x2, g.reshape(1, D_MODEL), b.reshape(1, D_MODEL))


_PROJ_OUTS = (
    (0, WIDTH_A, QK_SCALE, F32), (WIDTH_A, WIDTH_A, 1.0, F32), (2 * WIDTH_A, WIDTH_A, 1.0, F32),
    (3 * WIDTH_A, WIDTH_B, QK_SCALE, BF16), (3 * WIDTH_A + WIDTH_B, WIDTH_B, 1.0, BF16),
    (3 * WIDTH_A + 2 * WIDTH_B, WIDTH_B, 1.0, BF16), (3 * WIDTH_A + 3 * WIDTH_B, MIX_WIDTH, 1.0, BF16),
)


def _proj_kernel(h_ref, w_ref, *out_refs):
    h = h_ref[...]
    for (c0, width, scale, _), o_ref in zip(_PROJ_OUTS, out_refs):
        acc = jnp.dot(h, w_ref[:, c0:c0 + width], preferred_element_type=F32)
        if scale != 1.0:
            acc = acc * scale
        o_ref[...] = acc.astype(o_ref.dtype)


def _project(hb, w_bf16):
    n = hb.shape[0]
    return pl.pallas_call(
        _proj_kernel,
        grid=(n // ROW_TILE,),
        in_specs=[pl.BlockSpec((ROW_TILE, D_MODEL), lambda i: (i, 0)),
                  pl.BlockSpec((D_MODEL, IN_COLS), lambda i: (0, 0))],
        out_specs=[pl.BlockSpec((ROW_TILE, w), lambda i: (i, 0)) for _, w, _, _ in _PROJ_OUTS],
        out_shape=[jax.ShapeDtypeStruct((n, w), dt) for _, w, _, dt in _PROJ_OUTS],
        compiler_params=_params(("parallel",)),
        name="in_proj",
    )(hb, w_bf16)


def _split_pair(x):
    lane = lax.broadcasted_iota(jnp.int32, x.shape, 1)
    x = x.astype(F32)
    lo = jnp.where(lane < HEAD_DIM, x, 0.0).astype(BF16)
    hi = jnp.where(lane >= HEAD_DIM, x, 0.0).astype(BF16)
    return lo, hi


def _merge_pair(x0, x1):
    lane = lax.broadcasted_iota(jnp.int32, x0.shape, 1)
    return jnp.where(lane < HEAD_DIM, x0, x1)


_NT = (((1,), (1,)), ((), ()))


def _t5_bucket(dist):
    max_exact = N_BUCKETS // 2
    n = np.maximum(dist, 1).astype(np.float32)
    large = max_exact + (np.log(n / max_exact) / math.log(MAX_DISTANCE / max_exact)
                         * (N_BUCKETS - max_exact)).astype(np.int32)
    large = np.minimum(large, N_BUCKETS - 1)
    return np.where(dist < max_exact, dist, large).astype(np.int32)


def _bucket_segments(window, dil):
    buckets = _t5_bucket(np.arange(window // dil + 1) * dil)
    return [(d, int(bk)) for d, bk in enumerate(buckets) if d == 0 or bk != buckets[d - 1]]


def _bias_kernel(rel_ref, out_ref):
    i = lax.broadcasted_iota(jnp.int32, (BLK, 2 * BLK), 0)
    j = lax.broadcasted_iota(jnp.int32, (BLK, 2 * BLK), 1)
    for cfg, (window, dil) in enumerate(DILATED_CONFIGS):
        segments = _bucket_segments(window, dil)
        for variant in range(2):
            delta = (BLK + i - j) if variant == 0 else (i - j)
            band = (delta >= 0) & (delta <= window // dil)
            steps = [delta >= start for start, _ in segments[1:]]
            for h in range(N_HEADS_A):
                val = jnp.full((BLK, 2 * BLK), rel_ref[segments[0][1], h], F32)
                for step, (_, bucket) in zip(steps, segments[1:]):
                    val = jnp.where(step, rel_ref[bucket, h], val)
                out_ref[cfg, variant, h] = jnp.where(band, val, NEG_BIG)


def _bias_tables(rel_bias):
    shape = (len(DILATED_CONFIGS), 2, N_HEADS_A, BLK, 2 * BLK)
    return pl.pallas_call(
        _bias_kernel,
        in_specs=[pl.BlockSpec(memory_space=pltpu.SMEM)],
        out_shape=jax.ShapeDtypeStruct(shape, F32),
        compiler_params=pltpu.CompilerParams(vmem_limit_bytes=VMEM_LIMIT),
        name="bias_tables",
    )(rel_bias.astype(F32))


def _strided_rows(start, size, stride):
    return pl.ds(start, size) if stride == 1 else pl.ds(start, size, stride=stride)


def _dilated_kernel(q_ref, k_ref, v_ref, bias_ref, o_ref, m_ref, den_ref, num_ref):
    c = pl.program_id(2)
    last_cfg = len(DILATED_CONFIGS) - 1

    def load_block(cfg, dil, e):
        per_res = DIL_BLOCKS // dil
        r = e // per_res if isinstance(e, int) else lax.shift_right_logical(e, int(math.log2(per_res)))
        n = e - r * per_res
        g = c * per_res + n
        rows = _strided_rows(n * (BLK * dil) + r, BLK, dil)
        k_rows = _strided_rows(jnp.maximum(g - 1, 0) * (BLK * dil) + r, 2 * BLK, dil)
        blk = dict(rows=rows, first=jnp.where(g == 0, 1, 0), q=q_ref[rows, :],
                   k=k_ref[k_rows, :].astype(BF16), v=v_ref[k_rows, :].astype(BF16))
        if cfg > 0:
            blk.update(m=m_ref[rows, :], den=den_ref[rows, :], num=num_ref[rows, :])
        return blk

    def attend(cfg, blk):
        q_pair = _split_pair(blk["q"])
        ms, dens, nums = [], [], []
        for h in range(2):
            s = lax.dot_general(q_pair[h], blk["k"], _NT, preferred_element_type=F32)
            s = s + bias_ref[cfg, blk["first"], h]
            m = jnp.max(s, axis=-1, keepdims=True)
            p = jnp.exp(s - m)
            ms.append(jnp.broadcast_to(m, (BLK, LANES)))
            dens.append(jnp.broadcast_to(jnp.sum(p, axis=-1, keepdims=True), (BLK, LANES)))
            nums.append(jnp.dot(p.astype(BF16), blk["v"], preferred_element_type=F32))
        m, den, num = _merge_pair(*ms), _merge_pair(*dens), _merge_pair(*nums)
        if cfg > 0:
            m_new = jnp.maximum(blk["m"], m)
            w_old, w_new = jnp.exp(blk["m"] - m_new), jnp.exp(m - m_new)
            den = blk["den"] * w_old + den * w_new
            num = blk["num"] * w_old + num * w_new
            m = m_new
        return m, den, num

    for cfg, (_, dil) in enumerate(DILATED_CONFIGS):
        def group(it, carry, cfg=cfg, dil=dil):
            blocks = [load_block(cfg, dil, it * DIL_GROUP + u) for u in range(DIL_GROUP)]
            results = [attend(cfg, blk) for blk in blocks]
            for blk, (m, den, num) in zip(blocks, results):
                if cfg == last_cfg:
                    num_ref[blk["rows"], :] = num / den
                else:
                    m_ref[blk["rows"], :] = m
                    den_ref[blk["rows"], :] = den
                    num_ref[blk["rows"], :] = num
            return carry

        lax.fori_loop(0, DIL_BLOCKS // DIL_GROUP, group, 0)

    o_ref[...] = num_ref[...].astype(o_ref.dtype)


def _dilated(qa, ka, va, bias_tabs):
    b, t, _ = qa.shape
    assert t % DIL_CHUNK == 0
    q_spec = pl.BlockSpec((None, DIL_CHUNK, LANES), lambda bi, p, c: (bi, c, p))
    kv_spec = pl.BlockSpec((None, t, LANES), lambda bi, p, c: (bi, 0, p))
    bias_spec = pl.BlockSpec((len(DILATED_CONFIGS), 2, 2, BLK, 2 * BLK), lambda bi, p, c: (0, 0, p, 0, 0))
    return pl.pallas_call(
        _dilated_kernel,
        grid=(b, PAIRS_A, t // DIL_CHUNK),
        in_specs=[q_spec, kv_spec, kv_spec, bias_spec],
        out_specs=q_spec,
        out_shape=jax.ShapeDtypeStruct((b, t, WIDTH_A), BF16),
        scratch_shapes=[pltpu.VMEM((DIL_CHUNK, LANES), F32)] * 3,
        compiler_params=_params(("parallel", "parallel", "arbitrary")),
        name="dilated",
    )(qa, ka, va, bias_tabs)


def _softplus(z):
    return jnp.maximum(z, 0.0) + jnp.log(1.0 + jnp.exp2(jnp.abs(z) * (-LOG2E)))


def _sb_kernel(q_ref, k_ref, v_ref, tri_ref, o_ref, acc_ref, carry_ref, live_ref):
    i = pl.program_id(2)
    bq = q_ref.shape[0]
    q0, q1 = _split_pair(q_ref[...])
    qs = jnp.concatenate([q0, q1], axis=0)
    tri = tri_ref[...]

    def step(j, carry):
        diagonal = carry is None
        ks = pl.multiple_of(j * bq, bq)
        kb = k_ref[pl.ds(ks, bq), :]
        vb = v_ref[pl.ds(ks, bq), :]
        z = lax.dot_general(qs, kb, _NT, preferred_element_type=F32)
        t = _softplus(z)
        if diagonal:
            row = lax.broadcasted_iota(jnp.int32, z.shape, 0) & (bq - 1)
            colv = lax.broadcasted_iota(jnp.int32, z.shape, 1)
            causal = colv < row
            t = jnp.where(causal, t, 0.0)
        after = jnp.dot(t.astype(BF16), tri, preferred_element_type=F32)
        expo = (z - t) + after
        if not diagonal:
            expo = expo + jnp.concatenate([carry] * (bq // LANES), axis=1)
        a = jnp.exp2(expo * LOG2E)
        if diagonal:
            a = jnp.where(causal, a, 0.0)
        pv = jnp.dot(a.astype(BF16), vb, preferred_element_type=F32)
        block_sum = jnp.broadcast_to(after[:, 0:1] - t[:, 0:1], (2 * bq, LANES))
        return pv, block_sum

    def run(first_block, count, acc, carry):
        for u in range(count):
            pv, block_sum = step(first_block - u, carry)
            acc = pv if acc is None else acc + pv
            carry = block_sum if carry is None else carry + block_sum
        acc_ref[...], carry_ref[...] = acc, carry
        return jnp.max(carry)

    @pl.when(i == 0)
    def _():
        run(i, 1, None, None)
        live_ref[0] = SB_DEAD

    @pl.when(i > 0)
    def _():
        live_ref[0] = run(i, 2, None, None)

    def more(state):
        j, live = state
        return (j >= 1) & (live > SB_DEAD)

    def pair(state):
        j, _ = state
        return j - 2, run(j, 2, acc_ref[...], carry_ref[...])

    j, live = lax.while_loop(more, pair, (i - 2, live_ref[0]))

    @pl.when((j == 0) & (live > SB_DEAD))
    def _():
        run(0, 1, acc_ref[...], carry_ref[...])

    o_ref[...] = _merge_pair(acc_ref[0:bq, :], acc_ref[bq:2 * bq, :]).astype(o_ref.dtype)


def _stick_breaking(qb, kb, vb):
    b, t, _ = qb.shape
    bq = SB_BLOCK
    assert t % bq == 0
    r = np.arange(bq)
    tri = jnp.asarray(np.where(r[:, None] > r[None, :], -1.0, 0.0), dtype=BF16)
    q_spec = pl.BlockSpec((None, bq, LANES), lambda bi, p, i: (bi, i, p))
    kv_spec = pl.BlockSpec((None, t, LANES), lambda bi, p, i: (bi, 0, p))
    return pl.pallas_call(
        _sb_kernel,
        grid=(b, PAIRS_B, t // bq),
        in_specs=[q_spec, kv_spec, kv_spec, pl.BlockSpec((bq, bq), lambda bi, p, i: (0, 0))],
        out_specs=q_spec,
        out_shape=jax.ShapeDtypeStruct((b, t, WIDTH_B), BF16),
        scratch_shapes=[pltpu.VMEM((2 * bq, LANES), F32), pltpu.VMEM((2 * bq, LANES), F32),
                        pltpu.SMEM((1,), F32)],
        compiler_params=_params(("parallel", "parallel", "arbitrary")),
        name="stick_breaking",
    )(qb, kb, vb, tri)


def _out_kernel(ya_ref, yb_ref, gate_ref, h_ref, w_ref, b_ref, g_ref, beta_ref, hn_ref, hnb_ref):
    gate = gate_ref[...].astype(F32)
    silu = gate * (1.0 / (1.0 + jnp.exp(-gate)))
    ya = (ya_ref[...].astype(F32) * silu[:, :WIDTH_A]).astype(BF16)
    yb = (yb_ref[...].astype(F32) * silu[:, WIDTH_A:]).astype(BF16)
    out = (jnp.dot(ya, w_ref[0:WIDTH_A, :], preferred_element_type=F32)
           + jnp.dot(yb, w_ref[WIDTH_A:MIX_WIDTH, :], preferred_element_type=F32)
           + b_ref[...])
    y = _layer_norm_rows(DEEPNORM_ALPHA * h_ref[...] + out, g_ref[...], beta_ref[...])
    hn_ref[...] = y
    hnb_ref[...] = y.astype(BF16)


def _out_block(ya, yb, gate, h, w_bf16, b_out, g, beta):
    n = h.shape[0]
    half = pl.BlockSpec((ROW_TILE, WIDTH_A), lambda i: (i, 0))
    full = pl.BlockSpec((ROW_TILE, D_MODEL), lambda i: (i, 0))
    vec = pl.BlockSpec((1, D_MODEL), lambda i: (0, 0))
    return pl.pallas_call(
        _out_kernel,
        grid=(n // ROW_TILE,),
        in_specs=[half, half, full, full, pl.BlockSpec((MIX_WIDTH, D_MODEL), lambda i: (0, 0)), vec, vec, vec],
        out_specs=[full, full],
        out_shape=[jax.ShapeDtypeStruct((n, D_MODEL), F32), jax.ShapeDtypeStruct((n, D_MODEL), BF16)],
        compiler_params=_params(("parallel",)),
        name="out_proj_norm",
    )(ya.reshape(n, WIDTH_A), yb.reshape(n, WIDTH_B), gate, h, w_bf16,
      b_out.reshape(1, D_MODEL), g.reshape(1, D_MODEL), beta.reshape(1, D_MODEL))


def kernel(x, ln0_g, ln0_b, w_in, w_out, b_out, ln_g, ln_b, rel_bias):
    b, t, _ = x.shape
    n = b * t
    bias_tabs = _bias_tables(rel_bias)
    h, hb = _ln0(x.reshape(n, D_MODEL), ln0_g, ln0_b)
    for l in range(DEPTH):
        qa, ka, va, qb, kb, vb, gate = _project(hb, w_in[l].astype(BF16))
        seq = lambda a: a.reshape(b, t, a.shape[-1])
        ya = _dilated(seq(qa), seq(ka), seq(va), bias_tabs)
        yb = _stick_breaking(seq(qb), seq(kb), seq(vb))
        h, hb = _out_block(ya, yb, gate, h, w_out[l].astype(BF16), b_out[l], ln_g[l], ln_b[l])
    return h.reshape(b, t, D_MODEL)
```

```python
import functools
import math

import numpy as np
import jax
import jax.numpy as jnp
from jax import lax
from jax.experimental import pallas as pl
from jax.experimental.pallas import tpu as pltpu

D_MODEL = 1024
DEPTH = 2
HEAD_DIM = 64
N_HEADS_A = 8
N_HEADS_B = 8
WIDTH_A = N_HEADS_A * HEAD_DIM
WIDTH_B = N_HEADS_B * HEAD_DIM
MIX_WIDTH = WIDTH_A + WIDTH_B
IN_COLS = 3 * WIDTH_A + 3 * WIDTH_B + MIX_WIDTH
DILATED_CONFIGS = ((128, 1), (512, 4), (2048, 16))
N_CFG = len(DILATED_CONFIGS)
BLK = 128
N_BUCKETS = 32
MAX_DISTANCE = 2048
LN_EPS = 1e-5
DEEPNORM_ALPHA = (2.0 * DEPTH) ** 0.25
QK_SCALE = 1.0 / math.sqrt(HEAD_DIM)
LOG2E = math.log2(math.e)

LANES = 128
PAIRS_A = WIDTH_A // LANES
PAIRS_B = WIDTH_B // LANES
NEG_BIG = -1e30

ROW_TILE = 512
SB_BLOCK = 256
SB_SUBS = 2
SB_DEAD = -110.0
DIL_CHUNK = BLK * max(d for _, d in DILATED_CONFIGS)
DIL_BLOCKS = DIL_CHUNK // BLK
DIL_GROUP = 8
DIL_MERGE_ROWS = 256
VMEM_LIMIT = 48 * 1024 * 1024

F32 = jnp.float32
BF16 = jnp.bfloat16


def _params(semantics):
    return pltpu.CompilerParams(dimension_semantics=semantics, vmem_limit_bytes=VMEM_LIMIT)


def _layer_norm_rows(x, g, b):
    mu = jnp.mean(x, axis=-1, keepdims=True)
    xc = x - mu
    var = jnp.mean(xc * xc, axis=-1, keepdims=True)
    return xc * lax.rsqrt(var + LN_EPS) * g + b


_PROJ_OUTS = (
    (0, WIDTH_A, QK_SCALE, F32), (WIDTH_A, WIDTH_A, 1.0, F32), (2 * WIDTH_A, WIDTH_A, 1.0, F32),
    (3 * WIDTH_A, WIDTH_B, QK_SCALE, BF16), (3 * WIDTH_A + WIDTH_B, WIDTH_B, 1.0, BF16),
    (3 * WIDTH_A + 2 * WIDTH_B, WIDTH_B, 1.0, BF16), (3 * WIDTH_A + 3 * WIDTH_B, MIX_WIDTH, 1.0, BF16),
)


def _proj_kernel(*refs, normalize):
    if normalize:
        x_ref, g_ref, b_ref, w_ref, *out_refs = refs
        h = _layer_norm_rows(x_ref[...], g_ref[...], b_ref[...]).astype(BF16)
    else:
        h_ref, w_ref, *out_refs = refs
        h = h_ref[...]
    for (c0, width, scale, _), o_ref in zip(_PROJ_OUTS, out_refs):
        acc = jnp.dot(h, w_ref[:, c0:c0 + width], preferred_element_type=F32)
        if scale != 1.0:
            acc = acc * scale
        o_ref[...] = acc.astype(o_ref.dtype)


def _project(rows, w_bf16, ln0=None):
    n = rows.shape[0]
    row = pl.BlockSpec((ROW_TILE, D_MODEL), lambda i: (i, 0))
    vec = pl.BlockSpec((1, D_MODEL), lambda i: (0, 0))
    weights = pl.BlockSpec((D_MODEL, IN_COLS), lambda i: (0, 0))
    ln_args = () if ln0 is None else tuple(a.reshape(1, D_MODEL) for a in ln0)
    return pl.pallas_call(
        functools.partial(_proj_kernel, normalize=ln0 is not None),
        grid=(n // ROW_TILE,),
        in_specs=[row] + [vec] * len(ln_args) + [weights],
        out_specs=[pl.BlockSpec((ROW_TILE, w), lambda i: (i, 0)) for _, w, _, _ in _PROJ_OUTS],
        out_shape=[jax.ShapeDtypeStruct((n, w), dt) for _, w, _, dt in _PROJ_OUTS],
        compiler_params=_params(("parallel",)),
        name="in_proj",
    )(rows, *ln_args, w_bf16)


def _split_pair(x):
    lane = lax.broadcasted_iota(jnp.int32, x.shape, 1)
    x = x.astype(F32)
    lo = jnp.where(lane < HEAD_DIM, x, 0.0).astype(BF16)
    hi = jnp.where(lane >= HEAD_DIM, x, 0.0).astype(BF16)
    return lo, hi


def _merge_pair(x0, x1):
    lane = lax.broadcasted_iota(jnp.int32, x0.shape, 1)
    return jnp.where(lane < HEAD_DIM, x0, x1)


_NT = (((1,), (1,)), ((), ()))


def _t5_bucket(dist):
    max_exact = N_BUCKETS // 2
    n = np.maximum(dist, 1).astype(np.float32)
    large = max_exact + (np.log(n / max_exact) / math.log(MAX_DISTANCE / max_exact)
                         * (N_BUCKETS - max_exact)).astype(np.int32)
    large = np.minimum(large, N_BUCKETS - 1)
    return np.where(dist < max_exact, dist, large).astype(np.int32)


def _bucket_segments(window, dil):
    buckets = _t5_bucket(np.arange(window // dil + 1) * dil)
    return [(d, int(bk)) for d, bk in enumerate(buckets) if d == 0 or bk != buckets[d - 1]]


def _bias_kernel(rel_ref, out_ref):
    i = lax.broadcasted_iota(jnp.int32, (BLK, 2 * BLK), 0)
    j = lax.broadcasted_iota(jnp.int32, (BLK, 2 * BLK), 1)
    for cfg, (window, dil) in enumerate(DILATED_CONFIGS):
        segments = _bucket_segments(window, dil)
        for variant in range(2):
            delta = (BLK + i - j) if variant == 0 else (i - j)
            band = (delta >= 0) & (delta <= window // dil)
            steps = [delta >= start for start, _ in segments[1:]]
            for h in range(N_HEADS_A):
                val = jnp.full((BLK, 2 * BLK), rel_ref[segments[0][1], h], F32)
                for step, (_, bucket) in zip(steps, segments[1:]):
                    val = jnp.where(step, rel_ref[bucket, h], val)
                out_ref[cfg, variant, h] = jnp.where(band, val, NEG_BIG)


def _bias_tables(rel_bias):
    shape = (N_CFG, 2, N_HEADS_A, BLK, 2 * BLK)
    return pl.pallas_call(
        _bias_kernel,
        in_specs=[pl.BlockSpec(memory_space=pltpu.SMEM)],
        out_shape=jax.ShapeDtypeStruct(shape, F32),
        compiler_params=pltpu.CompilerParams(vmem_limit_bytes=VMEM_LIMIT),
        name="bias_tables",
    )(rel_bias.astype(F32))


def _strided_rows(start, size, stride):
    return pl.ds(start, size) if stride == 1 else pl.ds(start, size, stride=stride)


def _dilated_kernel(q_ref, k_ref, v_ref, bias_ref, ones_ref, o_ref, *state_refs):
    c = pl.program_id(2)
    m_refs, den_refs, num_refs = state_refs[:N_CFG], state_refs[N_CFG:2 * N_CFG], state_refs[2 * N_CFG:]

    def attend(cfg, dil, e):
        per_res = DIL_BLOCKS // dil
        r = lax.shift_right_logical(e, int(math.log2(per_res)))
        n = e - r * per_res
        g = c * per_res + n
        rows = _strided_rows(n * (BLK * dil) + r, BLK, dil)
        k_rows = _strided_rows(jnp.maximum(g - 1, 0) * (BLK * dil) + r, 2 * BLK, dil)
        first = jnp.where(g == 0, 1, 0)
        q_pair = _split_pair(q_ref[rows, :])
        kb = k_ref[k_rows, :].astype(BF16)
        v_sum = jnp.concatenate([jnp.concatenate(_split_pair(v_ref[k_rows, :]), axis=0), ones_ref[...]], axis=1)
        ms, ps = [], []
        for h in range(2):
            s = lax.dot_general(q_pair[h], kb, _NT, preferred_element_type=F32) + bias_ref[cfg, first, h]
            m = jnp.max(s, axis=-1, keepdims=True)
            ps.append(jnp.exp(s - m).astype(BF16))
            ms.append(jnp.broadcast_to(m, (BLK, LANES)))
        both = jnp.dot(jnp.concatenate(ps, axis=1), v_sum, preferred_element_type=F32)
        m_refs[cfg][rows, :] = _merge_pair(*ms)
        num_refs[cfg][rows, :] = both[:, :LANES]
        den_refs[cfg][rows, :] = both[:, LANES:]

    for cfg, (_, dil) in enumerate(DILATED_CONFIGS):
        def group(it, carry, cfg=cfg, dil=dil):
            for u in range(DIL_GROUP):
                attend(cfg, dil, it * DIL_GROUP + u)
            return carry

        lax.fori_loop(0, DIL_BLOCKS // DIL_GROUP, group, 0)

    def merge(it, carry):
        rows = pl.ds(pl.multiple_of(it * DIL_MERGE_ROWS, DIL_MERGE_ROWS), DIL_MERGE_ROWS)
        ms = [ref[rows, :] for ref in m_refs]
        top = functools.reduce(jnp.maximum, ms)
        ws = [jnp.exp(m - top) for m in ms]
        den = sum(w * ref[rows, :] for w, ref in zip(ws, den_refs))
        num = sum(w * ref[rows, :] for w, ref in zip(ws, num_refs))
        o_ref[rows, :] = (num / den).astype(o_ref.dtype)
        return carry

    lax.fori_loop(0, DIL_CHUNK // DIL_MERGE_ROWS, merge, 0)


def _dilated(qa, ka, va, bias_tabs):
    b, t, _ = qa.shape
    assert t % DIL_CHUNK == 0
    q_spec = pl.BlockSpec((None, DIL_CHUNK, LANES), lambda bi, p, c: (bi, c, p))
    kv_spec = pl.BlockSpec((None, t, LANES), lambda bi, p, c: (bi, 0, p))
    bias_spec = pl.BlockSpec((N_CFG, 2, 2, BLK, 2 * BLK), lambda bi, p, c: (0, 0, p, 0, 0))
    head_of_lane = np.arange(LANES) // HEAD_DIM
    head_of_row = np.arange(4 * BLK) // (2 * BLK)
    ones = jnp.asarray(head_of_row[:, None] == head_of_lane[None, :], dtype=BF16)
    return pl.pallas_call(
        _dilated_kernel,
        grid=(b, PAIRS_A, t // DIL_CHUNK),
        in_specs=[q_spec, kv_spec, kv_spec, bias_spec, pl.BlockSpec((4 * BLK, LANES), lambda bi, p, c: (0, 0))],
        out_specs=q_spec,
        out_shape=jax.ShapeDtypeStruct((b, t, WIDTH_A), BF16),
        scratch_shapes=[pltpu.VMEM((DIL_CHUNK, LANES), F32)] * (3 * N_CFG),
        compiler_params=_params(("parallel", "parallel", "arbitrary")),
        name="dilated",
    )(qa, ka, va, bias_tabs, ones)


def _softplus(z):
    return jnp.maximum(z, 0.0) + jnp.log(1.0 + jnp.exp2(jnp.abs(z) * (-LOG2E)))


def _sb_kernel(q_ref, k_ref, v_ref, tri_ref, o_ref, qs_ref, acc_ref, carry_ref, live_ref):
    step_id = pl.program_id(2)
    bq = SB_BLOCK
    for s in range(SB_SUBS):
        q0, q1 = _split_pair(q_ref[s * bq:(s + 1) * bq, :])
        qs_ref[s] = jnp.concatenate([q0, q1], axis=0)
    tri = tri_ref[...]

    def step(qs, j, carry):
        diagonal = carry is None
        ks = pl.multiple_of(j * bq, bq)
        kb = k_ref[pl.ds(ks, bq), :]
        vb = v_ref[pl.ds(ks, bq), :]
        z = lax.dot_general(qs, kb, _NT, preferred_element_type=F32)
        t = _softplus(z)
        if diagonal:
            row = lax.broadcasted_iota(jnp.int32, z.shape, 0) & (bq - 1)
            colv = lax.broadcasted_iota(jnp.int32, z.shape, 1)
            causal = colv < row
            t = jnp.where(causal, t, 0.0)
        after = jnp.dot(t.astype(BF16), tri, preferred_element_type=F32)
        expo = (z - t) + after
        if not diagonal:
            expo = expo + jnp.concatenate([carry] * (bq // LANES), axis=1)
        a = jnp.exp2(expo * LOG2E)
        if diagonal:
            a = jnp.where(causal, a, 0.0)
        pv = jnp.dot(a.astype(BF16), vb, preferred_element_type=F32)
        block_sum = jnp.broadcast_to(after[:, 0:1] - t[:, 0:1], (2 * bq, LANES))
        return pv, block_sum

    def run(s, first_block, count, fresh):
        qs = qs_ref[s]
        acc, carry = (None, None) if fresh else (acc_ref[s], carry_ref[s])
        for u in range(count):
            pv, block_sum = step(qs, first_block - u, carry)
            acc = pv if acc is None else acc + pv
            carry = block_sum if carry is None else carry + block_sum
        acc_ref[s], carry_ref[s] = acc, carry
        return jnp.max(carry)

    @pl.when(step_id == 0)
    def _():
        run(0, 0, 1, True)
        live_ref[0] = SB_DEAD
        for s in range(1, SB_SUBS):
            live_ref[s] = run(s, s, 2, True)

    @pl.when(step_id > 0)
    def _():
        for s in range(SB_SUBS):
            live_ref[s] = run(s, step_id * SB_SUBS + s, 2, True)

    def older(s, c):
        def more(state):
            j, live = state
            return (j >= 1) & (live > SB_DEAD)

        def pair(state):
            j, _ = state
            return j - 2, run(s, j, 2, False)

        j, live = lax.while_loop(more, pair, (step_id * SB_SUBS + s - 2, live_ref[s]))

        @pl.when((j == 0) & (live > SB_DEAD))
        def _():
            run(s, 0, 1, False)

        return c

    lax.fori_loop(0, SB_SUBS, older, 0)
    for s in range(SB_SUBS):
        o_ref[s * bq:(s + 1) * bq, :] = _merge_pair(acc_ref[s, 0:bq, :], acc_ref[s, bq:2 * bq, :]).astype(o_ref.dtype)


def _stick_breaking(qb, kb, vb):
    b, t, _ = qb.shape
    bq = SB_BLOCK
    rows = SB_SUBS * bq
    assert t % rows == 0
    r = np.arange(bq)
    tri = jnp.asarray(np.where(r[:, None] > r[None, :], -1.0, 0.0), dtype=BF16)
    q_spec = pl.BlockSpec((None, rows, LANES), lambda bi, p, i: (bi, i, p))
    kv_spec = pl.BlockSpec((None, t, LANES), lambda bi, p, i: (bi, 0, p))
    return pl.pallas_call(
        _sb_kernel,
        grid=(b, PAIRS_B, t // rows),
        in_specs=[q_spec, kv_spec, kv_spec, pl.BlockSpec((bq, bq), lambda bi, p, i: (0, 0))],
        out_specs=q_spec,
        out_shape=jax.ShapeDtypeStruct((b, t, WIDTH_B), BF16),
        scratch_shapes=[pltpu.VMEM((SB_SUBS, 2 * bq, LANES), BF16), pltpu.VMEM((SB_SUBS, 2 * bq, LANES), F32),
                        pltpu.VMEM((SB_SUBS, 2 * bq, LANES), F32), pltpu.SMEM((SB_SUBS,), F32)],
        compiler_params=_params(("parallel", "parallel", "arbitrary")),
        name="stick_breaking",
    )(qb, kb, vb, tri)


def _out_kernel(*refs, normalize, emit_bf16):
    ya_ref, yb_ref, gate_ref, h_ref, *refs = refs
    h = h_ref[...]
    if normalize:
        g0_ref, b0_ref, *refs = refs
        h = _layer_norm_rows(h, g0_ref[...], b0_ref[...])
    w_ref, b_ref, g_ref, beta_ref, hn_ref, *hnb_ref = refs
    gate = gate_ref[...].astype(F32)
    silu = gate * (1.0 / (1.0 + jnp.exp(-gate)))
    ya = (ya_ref[...].astype(F32) * silu[:, :WIDTH_A]).astype(BF16)
    yb = (yb_ref[...].astype(F32) * silu[:, WIDTH_A:]).astype(BF16)
    out = (jnp.dot(ya, w_ref[0:WIDTH_A, :], preferred_element_type=F32)
           + jnp.dot(yb, w_ref[WIDTH_A:MIX_WIDTH, :], preferred_element_type=F32)
           + b_ref[...])
    y = _layer_norm_rows(DEEPNORM_ALPHA * h + out, g_ref[...], beta_ref[...])
    hn_ref[...] = y
    if emit_bf16:
        hnb_ref[0][...] = y.astype(BF16)


def _out_block(ya, yb, gate, h, w_bf16, b_out, g, beta, ln0=None, emit_bf16=True):
    n = h.shape[0]
    half = pl.BlockSpec((ROW_TILE, WIDTH_A), lambda i: (i, 0))
    full = pl.BlockSpec((ROW_TILE, D_MODEL), lambda i: (i, 0))
    vec = pl.BlockSpec((1, D_MODEL), lambda i: (0, 0))
    as_vec = lambda a: a.reshape(1, D_MODEL)
    ln_args = () if ln0 is None else tuple(as_vec(a) for a in ln0)
    out_dtypes = (F32, BF16) if emit_bf16 else (F32,)
    return pl.pallas_call(
        functools.partial(_out_kernel, normalize=ln0 is not None, emit_bf16=emit_bf16),
        grid=(n // ROW_TILE,),
        in_specs=[half, half, full, full] + [vec] * len(ln_args)
                 + [pl.BlockSpec((MIX_WIDTH, D_MODEL), lambda i: (0, 0)), vec, vec, vec],
        out_specs=[full] * len(out_dtypes),
        out_shape=[jax.ShapeDtypeStruct((n, D_MODEL), dt) for dt in out_dtypes],
        compiler_params=_params(("parallel",)),
        name="out_proj_norm",
    )(ya.reshape(n, WIDTH_A), yb.reshape(n, WIDTH_B), gate, h, *ln_args, w_bf16,
      as_vec(b_out), as_vec(g), as_vec(beta))


def kernel(x, ln0_g, ln0_b, w_in, w_out, b_out, ln_g, ln_b, rel_bias):
    b, t, _ = x.shape
    n = b * t
    bias_tabs = _bias_tables(rel_bias)
    seq = lambda a: a.reshape(b, t, a.shape[-1])
    h, hb = x.reshape(n, D_MODEL), None
    for l in range(DEPTH):
        ln0 = (ln0_g, ln0_b) if l == 0 else None
        qa, ka, va, qb, kb, vb, gate = _project(h if l == 0 else hb, w_in[l].astype(BF16), ln0)
        ya = _dilated(seq(qa), seq(ka), seq(va), bias_tabs)
        yb = _stick_breaking(seq(qb), seq(kb), seq(vb))
        outs = _out_block(ya, yb, gate, h, w_out[l].astype(BF16), b_out[l], ln_g[l], ln_b[l],
                          ln0=ln0, emit_bf16=l < DEPTH - 1)
        h, hb = outs[0], (outs[1] if l < DEPTH - 1 else None)
    return h.reshape(b, t, D_MODEL)
```

```python
import functools
import math

import numpy as np
import jax
import jax.numpy as jnp
from jax import lax
from jax.experimental import pallas as pl
from jax.experimental.pallas import tpu as pltpu

D_MODEL = 1024
DEPTH = 2
HEAD_DIM = 64
N_HEADS_A = 8
N_HEADS_B = 8
WIDTH_A = N_HEADS_A * HEAD_DIM
WIDTH_B = N_HEADS_B * HEAD_DIM
MIX_WIDTH = WIDTH_A + WIDTH_B
IN_COLS = 3 * WIDTH_A + 3 * WIDTH_B + MIX_WIDTH
DILATED_CONFIGS = ((128, 1), (512, 4), (2048, 16))
N_CFG = len(DILATED_CONFIGS)
BLK = 128
N_BUCKETS = 32
MAX_DISTANCE = 2048
LN_EPS = 1e-5
DEEPNORM_ALPHA = (2.0 * DEPTH) ** 0.25
QK_SCALE = 1.0 / math.sqrt(HEAD_DIM)
LOG2E = math.log2(math.e)

LANES = 128
PAIRS_A = WIDTH_A // LANES
PAIRS_B = WIDTH_B // LANES
NEG_BIG = -1e30

ROW_TILE = 512
OUT_ROW_TILE = 1024
SB_BLOCK = 256
SB_SUBS = 4
SB_DEAD = -110.0
DIL_CHUNK = BLK * max(d for _, d in DILATED_CONFIGS)
DIL_BLOCKS = DIL_CHUNK // BLK
DIL_GROUP = (16, 16, 16)
VMEM_LIMIT = 48 * 1024 * 1024

F32 = jnp.float32
BF16 = jnp.bfloat16


def _params(semantics):
    return pltpu.CompilerParams(dimension_semantics=semantics, vmem_limit_bytes=VMEM_LIMIT)


def _layer_norm_rows(x, g, b):
    mu = jnp.mean(x, axis=-1, keepdims=True)
    xc = x - mu
    var = jnp.mean(xc * xc, axis=-1, keepdims=True)
    return xc * lax.rsqrt(var + LN_EPS) * g + b


_PROJ_OUTS = (
    (0, WIDTH_A, QK_SCALE, F32), (WIDTH_A, WIDTH_A, 1.0, F32), (2 * WIDTH_A, WIDTH_A, 1.0, F32),
    (3 * WIDTH_A, WIDTH_B, QK_SCALE, BF16), (3 * WIDTH_A + WIDTH_B, WIDTH_B, 1.0, BF16),
    (3 * WIDTH_A + 2 * WIDTH_B, WIDTH_B, 1.0, BF16), (3 * WIDTH_A + 3 * WIDTH_B, MIX_WIDTH, 1.0, BF16),
)


def _proj_kernel(*refs, normalize):
    if normalize:
        x_ref, g_ref, b_ref, w_ref, *out_refs = refs
        h = _layer_norm_rows(x_ref[...], g_ref[...], b_ref[...]).astype(BF16)
    else:
        h_ref, w_ref, *out_refs = refs
        h = h_ref[...]
    for (c0, width, scale, _), o_ref in zip(_PROJ_OUTS, out_refs):
        acc = jnp.dot(h, w_ref[:, c0:c0 + width], preferred_element_type=F32)
        if scale != 1.0:
            acc = acc * scale
        o_ref[...] = acc.astype(o_ref.dtype)


def _project(rows, w_bf16, ln0=None):
    n = rows.shape[0]
    row = pl.BlockSpec((ROW_TILE, D_MODEL), lambda i: (i, 0))
    vec = pl.BlockSpec((1, D_MODEL), lambda i: (0, 0))
    weights = pl.BlockSpec((D_MODEL, IN_COLS), lambda i: (0, 0))
    ln_args = () if ln0 is None else tuple(a.reshape(1, D_MODEL) for a in ln0)
    return pl.pallas_call(
        functools.partial(_proj_kernel, normalize=ln0 is not None),
        grid=(n // ROW_TILE,),
        in_specs=[row] + [vec] * len(ln_args) + [weights],
        out_specs=[pl.BlockSpec((ROW_TILE, w), lambda i: (i, 0)) for _, w, _, _ in _PROJ_OUTS],
        out_shape=[jax.ShapeDtypeStruct((n, w), dt) for _, w, _, dt in _PROJ_OUTS],
        compiler_params=_params(("parallel",)),
        name="in_proj",
    )(rows, *ln_args, w_bf16)


def _split_pair(x):
    lane = lax.broadcasted_iota(jnp.int32, x.shape, 1)
    x = x.astype(F32)
    lo = jnp.where(lane < HEAD_DIM, x, 0.0).astype(BF16)
    hi = jnp.where(lane >= HEAD_DIM, x, 0.0).astype(BF16)
    return lo, hi


def _merge_pair(x0, x1):
    lane = lax.broadcasted_iota(jnp.int32, x0.shape, 1)
    return jnp.where(lane < HEAD_DIM, x0, x1)


_NT = (((1,), (1,)), ((), ()))


def _t5_bucket(dist):
    max_exact = N_BUCKETS // 2
    n = np.maximum(dist, 1).astype(np.float32)
    large = max_exact + (np.log(n / max_exact) / math.log(MAX_DISTANCE / max_exact)
                         * (N_BUCKETS - max_exact)).astype(np.int32)
    large = np.minimum(large, N_BUCKETS - 1)
    return np.where(dist < max_exact, dist, large).astype(np.int32)


def _bucket_segments(window, dil):
    buckets = _t5_bucket(np.arange(window // dil + 1) * dil)
    return [(d, int(bk)) for d, bk in enumerate(buckets) if d == 0 or bk != buckets[d - 1]]


def _bias_kernel(rel_ref, out_ref):
    i = lax.broadcasted_iota(jnp.int32, (BLK, 2 * BLK), 0)
    j = lax.broadcasted_iota(jnp.int32, (BLK, 2 * BLK), 1)
    for cfg, (window, dil) in enumerate(DILATED_CONFIGS):
        segments = _bucket_segments(window, dil)
        for variant in range(2):
            delta = (BLK + i - j) if variant == 0 else (i - j)
            band = (delta >= 0) & (delta <= window // dil)
            steps = [delta >= start for start, _ in segments[1:]]
            for h in range(N_HEADS_A):
                val = jnp.full((BLK, 2 * BLK), rel_ref[segments[0][1], h], F32)
                for step, (_, bucket) in zip(steps, segments[1:]):
                    val = jnp.where(step, rel_ref[bucket, h], val)
                out_ref[cfg, variant, h] = jnp.where(band, val, NEG_BIG)


def _bias_tables(rel_bias):
    shape = (N_CFG, 2, N_HEADS_A, BLK, 2 * BLK)
    return pl.pallas_call(
        _bias_kernel,
        in_specs=[pl.BlockSpec(memory_space=pltpu.SMEM)],
        out_shape=jax.ShapeDtypeStruct(shape, F32),
        compiler_params=pltpu.CompilerParams(vmem_limit_bytes=VMEM_LIMIT),
        name="bias_tables",
    )(rel_bias.astype(F32))


def _strided_rows(start, size, stride):
    return pl.ds(start, size) if stride == 1 else pl.ds(start, size, stride=stride)


def _dilated_kernel(q_ref, k_ref, v_ref, bias_ref, ones_ref, o_ref, *state_refs):
    c = pl.program_id(2)
    n_state = N_CFG - 1
    m_refs, den_refs, num_refs = state_refs[:n_state], state_refs[n_state:2 * n_state], state_refs[2 * n_state:]

    def attend(cfg, dil, e):
        per_res = DIL_BLOCKS // dil
        r = 0 if dil == 1 else lax.shift_right_logical(e, int(math.log2(per_res)))
        n = e - r * per_res
        g = c * per_res + n
        align = (lambda x: pl.multiple_of(x, BLK)) if dil == 1 else (lambda x: x)
        rows = _strided_rows(align(n * (BLK * dil) + r), BLK, dil)
        k_rows = _strided_rows(align(jnp.maximum(g - 1, 0) * (BLK * dil) + r), 2 * BLK, dil)
        first = jnp.where(g == 0, 1, 0)
        q_pair = _split_pair(q_ref[rows, :])
        kb = k_ref[k_rows, :].astype(BF16)
        v_sum = jnp.concatenate([jnp.concatenate(_split_pair(v_ref[k_rows, :]), axis=0), ones_ref[...]], axis=1)
        ms, ps = [], []
        for h in range(2):
            s = lax.dot_general(q_pair[h], kb, _NT, preferred_element_type=F32) + bias_ref[cfg, first, h]
            m = jnp.max(s, axis=-1, keepdims=True)
            ps.append(jnp.exp(s - m).astype(BF16))
            ms.append(jnp.broadcast_to(m, (BLK, LANES)))
        both = jnp.dot(jnp.concatenate(ps, axis=1), v_sum, preferred_element_type=F32)
        m, num, den = _merge_pair(*ms), both[:, :LANES], both[:, LANES:]
        if cfg > 0:
            m_refs[cfg - 1][rows, :], num_refs[cfg - 1][rows, :], den_refs[cfg - 1][rows, :] = m, num, den
            return
        others = [ref[rows, :] for ref in m_refs]
        top = functools.reduce(jnp.maximum, others, m)
        w = jnp.exp(m - top)
        num, den = w * num, w * den
        for m_c, num_ref, den_ref in zip(others, num_refs, den_refs):
            w = jnp.exp(m_c - top)
            num, den = num + w * num_ref[rows, :], den + w * den_ref[rows, :]
        o_ref[rows, :] = (num / den).astype(o_ref.dtype)

    assert DILATED_CONFIGS[0][1] == 1
    for cfg in reversed(range(N_CFG)):
        def group(it, carry, cfg=cfg, dil=DILATED_CONFIGS[cfg][1], size=DIL_GROUP[cfg]):
            for u in range(size):
                attend(cfg, dil, it * size + u)
            return carry

        lax.fori_loop(0, DIL_BLOCKS // DIL_GROUP[cfg], group, 0)


def _dilated(qa, ka, va, bias_tabs):
    b, t, _ = qa.shape
    assert t % DIL_CHUNK == 0
    q_spec = pl.BlockSpec((None, DIL_CHUNK, LANES), lambda bi, p, c: (bi, c, p))
    kv_spec = pl.BlockSpec((None, t, LANES), lambda bi, p, c: (bi, 0, p))
    bias_spec = pl.BlockSpec((N_CFG, 2, 2, BLK, 2 * BLK), lambda bi, p, c: (0, 0, p, 0, 0))
    head_of_lane = np.arange(LANES) // HEAD_DIM
    head_of_row = np.arange(4 * BLK) // (2 * BLK)
    ones = jnp.asarray(head_of_row[:, None] == head_of_lane[None, :], dtype=BF16)
    return pl.pallas_call(
        _dilated_kernel,
        grid=(b, PAIRS_A, t // DIL_CHUNK),
        in_specs=[q_spec, kv_spec, kv_spec, bias_spec, pl.BlockSpec((4 * BLK, LANES), lambda bi, p, c: (0, 0))],
        out_specs=q_spec,
        out_shape=jax.ShapeDtypeStruct((b, t, WIDTH_A), BF16),
        scratch_shapes=[pltpu.VMEM((DIL_CHUNK, LANES), F32)] * (3 * (N_CFG - 1)),
        compiler_params=_params(("parallel", "parallel", "arbitrary")),
        name="dilated",
    )(qa, ka, va, bias_tabs, ones)


def _softplus(z):
    return jnp.maximum(z, 0.0) + jnp.log(1.0 + jnp.exp2(jnp.abs(z) * (-LOG2E)))


def _sb_kernel(q_ref, k_ref, v_ref, tri_ref, o_ref, qs_ref, acc_ref, carry_ref, live_ref):
    step_id = pl.program_id(2)
    bq = SB_BLOCK
    for s in range(SB_SUBS):
        q0, q1 = _split_pair(q_ref[s * bq:(s + 1) * bq, :])
        qs_ref[s] = jnp.concatenate([q0, q1], axis=0)
    tri = tri_ref[...]

    def step(qs, j, carry):
        diagonal = carry is None
        ks = pl.multiple_of(j * bq, bq)
        kb = k_ref[pl.ds(ks, bq), :]
        vb = v_ref[pl.ds(ks, bq), :]
        z = lax.dot_general(qs, kb, _NT, preferred_element_type=F32)
        t = _softplus(z)
        if diagonal:
            row = lax.broadcasted_iota(jnp.int32, z.shape, 0) & (bq - 1)
            colv = lax.broadcasted_iota(jnp.int32, z.shape, 1)
            causal = colv < row
            t = jnp.where(causal, t, 0.0)
        after = jnp.dot(t.astype(BF16), tri, preferred_element_type=F32)
        expo = (z - t) + after
        if not diagonal:
            expo = expo + jnp.concatenate([carry] * (bq // LANES), axis=1)
        a = jnp.exp2(expo * LOG2E)
        if diagonal:
            a = jnp.where(causal, a, 0.0)
        pv = jnp.dot(a.astype(BF16), vb, preferred_element_type=F32)
        block_sum = jnp.broadcast_to(after[:, 0:1] - t[:, 0:1], (2 * bq, LANES))
        return pv, block_sum

    def run(s, first_block, count, fresh):
        qs = qs_ref[s]
        acc, carry = (None, None) if fresh else (acc_ref[s], carry_ref[s])
        for u in range(count):
            pv, block_sum = step(qs, first_block - u, carry)
            acc = pv if acc is None else acc + pv
            carry = block_sum if carry is None else carry + block_sum
        acc_ref[s], carry_ref[s] = acc, carry
        return jnp.max(carry)

    @pl.when(step_id == 0)
    def _():
        run(0, 0, 1, True)
        live_ref[0] = SB_DEAD
        for s in range(1, SB_SUBS):
            live_ref[s] = run(s, s, 2, True)

    @pl.when(step_id > 0)
    def _():
        for s in range(SB_SUBS):
            live_ref[s] = run(s, step_id * SB_SUBS + s, 2, True)

    def older(s, c):
        def more(state):
            j, live = state
            return (j >= 1) & (live > SB_DEAD)

        def pair(state):
            j, _ = state
            return j - 2, run(s, j, 2, False)

        j, live = lax.while_loop(more, pair, (step_id * SB_SUBS + s - 2, live_ref[s]))

        @pl.when((j == 0) & (live > SB_DEAD))
        def _():
            run(s, 0, 1, False)

        return c

    lax.fori_loop(0, SB_SUBS, older, 0)
    for s in range(SB_SUBS):
        o_ref[s * bq:(s + 1) * bq, :] = _merge_pair(acc_ref[s, 0:bq, :], acc_ref[s, bq:2 * bq, :]).astype(o_ref.dtype)


def _stick_breaking(qb, kb, vb):
    b, t, _ = qb.shape
    bq = SB_BLOCK
    rows = SB_SUBS * bq
    assert t % rows == 0
    r = np.arange(bq)
    tri = jnp.asarray(np.where(r[:, None] > r[None, :], -1.0, 0.0), dtype=BF16)
    q_spec = pl.BlockSpec((None, rows, LANES), lambda bi, p, i: (bi, i, p))
    kv_spec = pl.BlockSpec((None, t, LANES), lambda bi, p, i: (bi, 0, p))
    return pl.pallas_call(
        _sb_kernel,
        grid=(b, PAIRS_B, t // rows),
        in_specs=[q_spec, kv_spec, kv_spec, pl.BlockSpec((bq, bq), lambda bi, p, i: (0, 0))],
        out_specs=q_spec,
        out_shape=jax.ShapeDtypeStruct((b, t, WIDTH_B), BF16),
        scratch_shapes=[pltpu.VMEM((SB_SUBS, 2 * bq, LANES), BF16), pltpu.VMEM((SB_SUBS, 2 * bq, LANES), F32),
                        pltpu.VMEM((SB_SUBS, 2 * bq, LANES), F32), pltpu.SMEM((SB_SUBS,), F32)],
        compiler_params=_params(("parallel", "parallel", "arbitrary")),
        name="stick_breaking",
    )(qb, kb, vb, tri)


def _out_kernel(*refs, normalize, emit_bf16):
    ya_ref, yb_ref, gate_ref, h_ref, *refs = refs
    h = h_ref[...]
    if normalize:
        g0_ref, b0_ref, *refs = refs
        h = _layer_norm_rows(h, g0_ref[...], b0_ref[...])
    w_ref, b_ref, g_ref, beta_ref, hn_ref, *hnb_ref = refs
    half_gate = 0.5 * gate_ref[...].astype(F32)
    silu = half_gate + half_gate * jnp.tanh(half_gate)
    ya = (ya_ref[...].astype(F32) * silu[:, :WIDTH_A]).astype(BF16)
    yb = (yb_ref[...].astype(F32) * silu[:, WIDTH_A:]).astype(BF16)
    out = (jnp.dot(ya, w_ref[0:WIDTH_A, :], preferred_element_type=F32)
           + jnp.dot(yb, w_ref[WIDTH_A:MIX_WIDTH, :], preferred_element_type=F32)
           + b_ref[...])
    y = _layer_norm_rows(DEEPNORM_ALPHA * h + out, g_ref[...], beta_ref[...])
    hn_ref[...] = y
    if emit_bf16:
        hnb_ref[0][...] = y.astype(BF16)


def _out_block(ya, yb, gate, h, w_bf16, b_out, g, beta, ln0=None, emit_bf16=True):
    n = h.shape[0]
    half = pl.BlockSpec((OUT_ROW_TILE, WIDTH_A), lambda i: (i, 0))
    full = pl.BlockSpec((OUT_ROW_TILE, D_MODEL), lambda i: (i, 0))
    vec = pl.BlockSpec((1, D_MODEL), lambda i: (0, 0))
    as_vec = lambda a: a.reshape(1, D_MODEL)
    ln_args = () if ln0 is None else tuple(as_vec(a) for a in ln0)
    out_dtypes = (F32, BF16) if emit_bf16 else (F32,)
    return pl.pallas_call(
        functools.partial(_out_kernel, normalize=ln0 is not None, emit_bf16=emit_bf16),
        grid=(n // OUT_ROW_TILE,),
        in_specs=[half, half, full, full] + [vec] * len(ln_args)
                 + [pl.BlockSpec((MIX_WIDTH, D_MODEL), lambda i: (0, 0)), vec, vec, vec],
        out_specs=[full] * len(out_dtypes),
        out_shape=[jax.ShapeDtypeStruct((n, D_MODEL), dt) for dt in out_dtypes],
        compiler_params=_params(("parallel",)),
        name="out_proj_norm",
    )(ya.reshape(n, WIDTH_A), yb.reshape(n, WIDTH_B), gate, h, *ln_args, w_bf16,
      as_vec(b_out), as_vec(g), as_vec(beta))


def kernel(x, ln0_g, ln0_b, w_in, w_out, b_out, ln_g, ln_b, rel_bias):
    b, t, _ = x.shape
    n = b * t
    bias_tabs = _bias_tables(rel_bias)
    seq = lambda a: a.reshape(b, t, a.shape[-1])
    h, hb = x.reshape(n, D_MODEL), None
    for l in range(DEPTH):
        ln0 = (ln0_g, ln0_b) if l == 0 else None
        qa, ka, va, qb, kb, vb, gate = _project(h if l == 0 else hb, w_in[l].astype(BF16), ln0)
        ya = _dilated(seq(qa), seq(ka), seq(va), bias_tabs)
        yb = _stick_breaking(seq(qb), seq(kb), seq(vb))
        outs = _out_block(ya, yb, gate, h, w_out[l].astype(BF16), b_out[l], ln_g[l], ln_b[l],
                          ln0=ln0, emit_bf16=l < DEPTH - 1)
        h, hb = outs[0], (outs[1] if l < DEPTH - 1 else None)
    return h.reshape(b, t, D_MODEL)
```

```python
import functools
import math

import numpy as np
import jax
import jax.numpy as jnp
from jax import lax
from jax.experimental import pallas as pl
from jax.experimental.pallas import tpu as pltpu

D_MODEL = 1024
DEPTH = 2
HEAD_DIM = 64
N_HEADS_A = 8
N_HEADS_B = 8
WIDTH_A = N_HEADS_A * HEAD_DIM
WIDTH_B = N_HEADS_B * HEAD_DIM
MIX_WIDTH = WIDTH_A + WIDTH_B
IN_COLS = 3 * WIDTH_A + 3 * WIDTH_B + MIX_WIDTH
DILATED_CONFIGS = ((128, 1), (512, 4), (2048, 16))
N_CFG = len(DILATED_CONFIGS)
BLK = 128
N_BUCKETS = 32
MAX_DISTANCE = 2048
LN_EPS = 1e-5
DEEPNORM_ALPHA = (2.0 * DEPTH) ** 0.25
QK_SCALE = 1.0 / math.sqrt(HEAD_DIM)
LOG2E = math.log2(math.e)

LANES = 128
PAIRS_A = WIDTH_A // LANES
PAIRS_B = WIDTH_B // LANES
NEG_BIG = -1e30

ROW_TILE = 512
OUT_ROW_TILE = 1024
SB_BLOCK = 256
SB_SUBS = 4
SB_DEAD = -110.0
DIL_CHUNK = BLK * max(d for _, d in DILATED_CONFIGS)
DIL_BLOCKS = DIL_CHUNK // BLK
VMEM_LIMIT = 48 * 1024 * 1024

F32 = jnp.float32
BF16 = jnp.bfloat16


def _params(semantics):
    return pltpu.CompilerParams(dimension_semantics=semantics, vmem_limit_bytes=VMEM_LIMIT)


def _layer_norm_rows(x, g, b):
    mu = jnp.mean(x, axis=-1, keepdims=True)
    xc = x - mu
    var = jnp.mean(xc * xc, axis=-1, keepdims=True)
    return xc * lax.rsqrt(var + LN_EPS) * g + b


_PROJ_A = ((0, QK_SCALE * LOG2E), (WIDTH_A, 1.0), (2 * WIDTH_A, 1.0))
_PROJ_B = ((3 * WIDTH_A, WIDTH_B, QK_SCALE), (3 * WIDTH_A + WIDTH_B, WIDTH_B, 1.0),
           (3 * WIDTH_A + 2 * WIDTH_B, WIDTH_B, 1.0), (3 * WIDTH_A + 3 * WIDTH_B, MIX_WIDTH, 1.0))
_DILS = tuple(d for _, d in DILATED_CONFIGS)


def _proj_kernel(*refs, normalize):
    if normalize:
        x_ref, g_ref, b_ref, w_ref, *refs = refs
        h = _layer_norm_rows(x_ref[...], g_ref[...], b_ref[...]).astype(BF16)
    else:
        h_ref, w_ref, *refs = refs
        h = h_ref[...]
    n_a = len(_PROJ_A) * N_CFG
    a_refs, b_refs, fold_refs = refs[:n_a], refs[n_a:n_a + len(_PROJ_B)], refs[n_a + len(_PROJ_B):]
    slabs = range(WIDTH_A // LANES)
    for t, (c0, scale) in enumerate(_PROJ_A):
        acc = jnp.dot(h, w_ref[:, c0:c0 + WIDTH_A], preferred_element_type=F32)
        if scale != 1.0:
            acc = acc * scale
        a_refs[t * N_CFG][0] = acc.astype(BF16)
        src_ref, src_dil = fold_refs[2 * t], 1
        for j in slabs:
            src_ref[j] = acc[:, j * LANES:(j + 1) * LANES]
        for cfg in range(1, N_CFG):
            dil, dst_ref = _DILS[cfg], fold_refs[2 * t + cfg % 2]
            step, rows_out = dil // src_dil, ROW_TILE // dil
            o_ref = a_refs[t * N_CFG + cfg]
            for r_src in range(src_dil):
                for a in range(step):
                    r = r_src + src_dil * a
                    rows = pl.ds(r_src * (ROW_TILE // src_dil) + a, rows_out, stride=step)
                    parts = [src_ref[j, rows, :] for j in slabs]
                    o_ref[r] = jnp.concatenate(parts, axis=1).astype(BF16)
                    if cfg + 1 < N_CFG:
                        for j in slabs:
                            dst_ref[j, r * rows_out:(r + 1) * rows_out, :] = parts[j]
            src_ref, src_dil = dst_ref, dil
    for (c0, width, scale), o_ref in zip(_PROJ_B, b_refs):
        acc = jnp.dot(h, w_ref[:, c0:c0 + width], preferred_element_type=F32)
        if scale != 1.0:
            acc = acc * scale
        o_ref[...] = acc.astype(o_ref.dtype)


def _project(rows, w_bf16, batch, ln0=None):
    n = rows.shape[0]
    t = n // batch
    tiles = t // ROW_TILE
    assert t % ROW_TILE == 0 and all(ROW_TILE % (16 * d) == 0 for d in _DILS)
    row = pl.BlockSpec((ROW_TILE, D_MODEL), lambda i: (i, 0))
    vec = pl.BlockSpec((1, D_MODEL), lambda i: (0, 0))
    weights = pl.BlockSpec((D_MODEL, IN_COLS), lambda i: (0, 0))
    ln_args = () if ln0 is None else tuple(a.reshape(1, D_MODEL) for a in ln0)
    a_specs = [pl.BlockSpec((None, d, ROW_TILE // d, WIDTH_A), lambda i: (i // tiles, 0, i % tiles, 0))
               for _ in _PROJ_A for d in _DILS]
    a_shapes = [jax.ShapeDtypeStruct((batch, d, t // d, WIDTH_A), BF16) for _ in _PROJ_A for d in _DILS]
    outs = pl.pallas_call(
        functools.partial(_proj_kernel, normalize=ln0 is not None),
        grid=(n // ROW_TILE,),
        in_specs=[row] + [vec] * len(ln_args) + [weights],
        out_specs=a_specs + [pl.BlockSpec((ROW_TILE, w), lambda i: (i, 0)) for _, w, _ in _PROJ_B],
        out_shape=a_shapes + [jax.ShapeDtypeStruct((n, w), BF16) for _, w, _ in _PROJ_B],
        scratch_shapes=[pltpu.VMEM((WIDTH_A // LANES, ROW_TILE, LANES), F32)] * (2 * len(_PROJ_A)),
        compiler_params=_params(("parallel",)),
        name="in_proj",
    )(rows, *ln_args, w_bf16)
    n_a = len(_PROJ_A) * N_CFG
    mixer_a = [outs[i * N_CFG:(i + 1) * N_CFG] for i in range(len(_PROJ_A))]
    return mixer_a, outs[n_a:]


def _split_pair(x):
    lane = lax.broadcasted_iota(jnp.int32, x.shape, 1)
    x = x.astype(F32)
    lo = jnp.where(lane < HEAD_DIM, x, 0.0).astype(BF16)
    hi = jnp.where(lane >= HEAD_DIM, x, 0.0).astype(BF16)
    return lo, hi


def _merge_pair(x0, x1):
    lane = lax.broadcasted_iota(jnp.int32, x0.shape, 1)
    return jnp.where(lane < HEAD_DIM, x0, x1)


_NT = (((1,), (1,)), ((), ()))


def _t5_bucket(dist):
    max_exact = N_BUCKETS // 2
    n = np.maximum(dist, 1).astype(np.float32)
    large = max_exact + (np.log(n / max_exact) / math.log(MAX_DISTANCE / max_exact)
                         * (N_BUCKETS - max_exact)).astype(np.int32)
    large = np.minimum(large, N_BUCKETS - 1)
    return np.where(dist < max_exact, dist, large).astype(np.int32)


def _bucket_segments(window, dil):
    buckets = _t5_bucket(np.arange(window // dil + 1) * dil)
    return [(d, int(bk)) for d, bk in enumerate(buckets) if d == 0 or bk != buckets[d - 1]]


def _bias_kernel(rel_ref, out_ref):
    i = lax.broadcasted_iota(jnp.int32, (BLK, 2 * BLK), 0)
    j = lax.broadcasted_iota(jnp.int32, (BLK, 2 * BLK), 1)
    for cfg, (window, dil) in enumerate(DILATED_CONFIGS):
        segments = _bucket_segments(window, dil)
        for variant in range(2):
            delta = (BLK + i - j) if variant == 0 else (i - j)
            band = (delta >= 0) & (delta <= window // dil)
            steps = [delta >= start for start, _ in segments[1:]]
            for h in range(N_HEADS_A):
                val = jnp.full((BLK, 2 * BLK), rel_ref[segments[0][1], h] * LOG2E, F32)
                for step, (_, bucket) in zip(steps, segments[1:]):
                    val = jnp.where(step, rel_ref[bucket, h] * LOG2E, val)
                out_ref[cfg, variant, h] = jnp.where(band, val, NEG_BIG)


def _bias_tables(rel_bias):
    shape = (N_CFG, 2, N_HEADS_A, BLK, 2 * BLK)
    return pl.pallas_call(
        _bias_kernel,
        in_specs=[pl.BlockSpec(memory_space=pltpu.SMEM)],
        out_shape=jax.ShapeDtypeStruct(shape, F32),
        compiler_params=pltpu.CompilerParams(vmem_limit_bytes=VMEM_LIMIT),
        name="bias_tables",
    )(rel_bias.astype(F32))


def _dilated_kernel(*refs):
    q_refs, k_refs, v_refs = refs[:N_CFG], refs[N_CFG:2 * N_CFG], refs[2 * N_CFG:3 * N_CFG]
    bias_ref, ones_ref, o_ref, *state_refs = refs[3 * N_CFG:]
    c = pl.program_id(2)
    n_state = N_CFG - 1
    m_refs, den_refs, num_refs = state_refs[:n_state], state_refs[n_state:2 * n_state], state_refs[2 * n_state:]

    def attend(cfg, dil, r, n):
        per_res = DIL_BLOCKS // dil
        g = c * per_res + n
        k_rows = pl.ds(pl.multiple_of(jnp.maximum(g - 1, 0) * BLK, BLK), 2 * BLK)
        first = jnp.where(g == 0, 1, 0) if n == 0 else 0
        ones = ones_ref[...]
        q = q_refs[cfg][r, n * BLK:(n + 1) * BLK, :]
        q_pair = (q * ones[0:BLK], q * ones[2 * BLK:3 * BLK])
        kb = k_refs[cfg][r, k_rows, :]
        vb = v_refs[cfg][r, k_rows, :]
        v_sum = jnp.concatenate([jnp.concatenate([vb, vb], axis=0) * ones, ones], axis=1)
        ms, ps = [], []
        for h in range(2):
            s = lax.dot_general(q_pair[h], kb, _NT, preferred_element_type=F32) + bias_ref[cfg, first, h]
            m = jnp.max(s, axis=-1, keepdims=True)
            ps.append(jnp.exp2(s - m).astype(BF16))
            ms.append(jnp.broadcast_to(m, (BLK, LANES)))
        both = jnp.dot(jnp.concatenate(ps, axis=1), v_sum, preferred_element_type=F32)
        m, num, den = _merge_pair(*ms), both[:, :LANES], both[:, LANES:]
        if cfg > 0:
            rows = pl.ds(n * BLK * dil + r, BLK, stride=dil)
            m_refs[cfg - 1][rows, :], num_refs[cfg - 1][rows, :], den_refs[cfg - 1][rows, :] = m, num, den
            return
        rows = pl.ds(n * BLK, BLK)
        others = [ref[rows, :] for ref in m_refs]
        top = functools.reduce(jnp.maximum, others, m)
        w = jnp.exp2(m - top)
        num, den = w * num, w * den
        for m_c, num_ref, den_ref in zip(others, num_refs, den_refs):
            w = jnp.exp2(m_c - top)
            num, den = num + w * num_ref[rows, :], den + w * den_ref[rows, :]
        o_ref[rows, :] = (num / den).astype(o_ref.dtype)

    assert _DILS[0] == 1
    for cfg in reversed(range(N_CFG)):
        dil = _DILS[cfg]
        for r in range(dil):
            for n in range(DIL_BLOCKS // dil):
                attend(cfg, dil, r, n)


def _dilated(qs, ks, vs, bias_tabs):
    b, _, t, _ = qs[0].shape
    assert t % DIL_CHUNK == 0
    q_specs = [pl.BlockSpec((None, d, DIL_CHUNK // d, LANES), lambda bi, p, c: (bi, 0, c, p)) for d in _DILS]
    kv_specs = [pl.BlockSpec((None, d, t // d, LANES), lambda bi, p, c: (bi, 0, 0, p)) for d in _DILS]
    bias_spec = pl.BlockSpec((N_CFG, 2, 2, BLK, 2 * BLK), lambda bi, p, c: (0, 0, p, 0, 0))
    head_of_lane = np.arange(LANES) // HEAD_DIM
    head_of_row = np.arange(4 * BLK) // (2 * BLK)
    ones = jnp.asarray(head_of_row[:, None] == head_of_lane[None, :], dtype=BF16)
    return pl.pallas_call(
        _dilated_kernel,
        grid=(b, PAIRS_A, t // DIL_CHUNK),
        in_specs=q_specs + kv_specs + kv_specs
                 + [bias_spec, pl.BlockSpec((4 * BLK, LANES), lambda bi, p, c: (0, 0))],
        out_specs=pl.BlockSpec((None, DIL_CHUNK, LANES), lambda bi, p, c: (bi, c, p)),
        out_shape=jax.ShapeDtypeStruct((b, t, WIDTH_A), BF16),
        scratch_shapes=[pltpu.VMEM((DIL_CHUNK, LANES), F32)] * (3 * (N_CFG - 1)),
        compiler_params=_params(("parallel", "parallel", "arbitrary")),
        name="dilated",
    )(*qs, *ks, *vs, bias_tabs, ones)


def _softplus(z):
    return jnp.maximum(z, 0.0) + jnp.log(1.0 + jnp.exp2(jnp.abs(z) * (-LOG2E)))


def _sb_kernel(q_ref, k_ref, v_ref, tri_ref, o_ref, qs_ref, acc_ref, carry_ref, live_ref):
    step_id = pl.program_id(2)
    bq = SB_BLOCK
    for s in range(SB_SUBS):
        q0, q1 = _split_pair(q_ref[s * bq:(s + 1) * bq, :])
        qs_ref[s] = jnp.concatenate([q0, q1], axis=0)
    tri = tri_ref[...]

    def step(qs, j, carry):
        diagonal = carry is None
        ks = pl.multiple_of(j * bq, bq)
        kb = k_ref[pl.ds(ks, bq), :]
        vb = v_ref[pl.ds(ks, bq), :]
        z = lax.dot_general(qs, kb, _NT, preferred_element_type=F32)
        t = _softplus(z)
        if diagonal:
            row = lax.broadcasted_iota(jnp.int32, z.shape, 0) & (bq - 1)
            colv = lax.broadcasted_iota(jnp.int32, z.shape, 1)
            causal = colv < row
            t = jnp.where(causal, t, 0.0)
        after = jnp.dot(t.astype(BF16), tri, preferred_element_type=F32)
        expo = (z - t) + after
        if not diagonal:
            expo = expo + jnp.concatenate([carry] * (bq // LANES), axis=1)
        a = jnp.exp2(expo * LOG2E)
        if diagonal:
            a = jnp.where(causal, a, 0.0)
        pv = jnp.dot(a.astype(BF16), vb, preferred_element_type=F32)
        block_sum = jnp.broadcast_to(after[:, 0:1] - t[:, 0:1], (2 * bq, LANES))
        return pv, block_sum

    def run(s, first_block, count, fresh):
        qs = qs_ref[s]
        acc, carry = (None, None) if fresh else (acc_ref[s], carry_ref[s])
        for u in range(count):
            pv, block_sum = step(qs, first_block - u, carry)
            acc = pv if acc is None else acc + pv
            carry = block_sum if carry is None else carry + block_sum
        acc_ref[s], carry_ref[s] = acc, carry
        return jnp.max(carry)

    @pl.when(step_id == 0)
    def _():
        run(0, 0, 1, True)
        live_ref[0] = SB_DEAD
        for s in range(1, SB_SUBS):
            live_ref[s] = run(s, s, 2, True)

    @pl.when(step_id > 0)
    def _():
        for s in range(SB_SUBS):
            live_ref[s] = run(s, step_id * SB_SUBS + s, 2, True)

    def older(s, c):
        def more(state):
            j, live = state
            return (j >= 1) & (live > SB_DEAD)

        def pair(state):
            j, _ = state
            return j - 2, run(s, j, 2, False)

        j, live = lax.while_loop(more, pair, (step_id * SB_SUBS + s - 2, live_ref[s]))

        @pl.when((j == 0) & (live > SB_DEAD))
        def _():
            run(s, 0, 1, False)

        return c

    lax.fori_loop(0, SB_SUBS, older, 0)
    for s in range(SB_SUBS):
        o_ref[s * bq:(s + 1) * bq, :] = _merge_pair(acc_ref[s, 0:bq, :], acc_ref[s, bq:2 * bq, :]).astype(o_ref.dtype)


def _stick_breaking(qb, kb, vb):
    b, t, _ = qb.shape
    bq = SB_BLOCK
    rows = SB_SUBS * bq
    assert t % rows == 0
    r = np.arange(bq)
    tri = jnp.asarray(np.where(r[:, None] > r[None, :], -1.0, 0.0), dtype=BF16)
    q_spec = pl.BlockSpec((None, rows, LANES), lambda bi, p, i: (bi, i, p))
    kv_spec = pl.BlockSpec((None, t, LANES), lambda bi, p, i: (bi, 0, p))
    return pl.pallas_call(
        _sb_kernel,
        grid=(b, PAIRS_B, t // rows),
        in_specs=[q_spec, kv_spec, kv_spec, pl.BlockSpec((bq, bq), lambda bi, p, i: (0, 0))],
        out_specs=q_spec,
        out_shape=jax.ShapeDtypeStruct((b, t, WIDTH_B), BF16),
        scratch_shapes=[pltpu.VMEM((SB_SUBS, 2 * bq, LANES), BF16), pltpu.VMEM((SB_SUBS, 2 * bq, LANES), F32),
                        pltpu.VMEM((SB_SUBS, 2 * bq, LANES), F32), pltpu.SMEM((SB_SUBS,), F32)],
        compiler_params=_params(("parallel", "parallel", "arbitrary")),
        name="stick_breaking",
    )(qb, kb, vb, tri)


def _out_kernel(*refs, normalize, emit_bf16):
    ya_ref, yb_ref, gate_ref, h_ref, *refs = refs
    h = h_ref[...]
    if normalize:
        g0_ref, b0_ref, *refs = refs
        h = _layer_norm_rows(h, g0_ref[...], b0_ref[...])
    w_ref, b_ref, g_ref, beta_ref, hn_ref, *hnb_ref = refs
    half_gate = 0.5 * gate_ref[...].astype(F32)
    silu = half_gate + half_gate * jnp.tanh(half_gate)
    ya = (ya_ref[...].astype(F32) * silu[:, :WIDTH_A]).astype(BF16)
    yb = (yb_ref[...].astype(F32) * silu[:, WIDTH_A:]).astype(BF16)
    out = (jnp.dot(ya, w_ref[0:WIDTH_A, :], preferred_element_type=F32)
           + jnp.dot(yb, w_ref[WIDTH_A:MIX_WIDTH, :], preferred_element_type=F32)
           + b_ref[...])
    y = _layer_norm_rows(DEEPNORM_ALPHA * h + out, g_ref[...], beta_ref[...])
    hn_ref[...] = y
    if emit_bf16:
        hnb_ref[0][...] = y.astype(BF16)


def _out_block(ya, yb, gate, h, w_bf16, b_out, g, beta, ln0=None, emit_bf16=True):
    n = h.shape[0]
    half = pl.BlockSpec((OUT_ROW_TILE, WIDTH_A), lambda i: (i, 0))
    full = pl.BlockSpec((OUT_ROW_TILE, D_MODEL), lambda i: (i, 0))
    vec = pl.BlockSpec((1, D_MODEL), lambda i: (0, 0))
    as_vec = lambda a: a.reshape(1, D_MODEL)
    ln_args = () if ln0 is None else tuple(as_vec(a) for a in ln0)
    out_dtypes = (F32, BF16) if emit_bf16 else (F32,)
    return pl.pallas_call(
        functools.partial(_out_kernel, normalize=ln0 is not None, emit_bf16=emit_bf16),
        grid=(n // OUT_ROW_TILE,),
        in_specs=[half, half, full, full] + [vec] * len(ln_args)
                 + [pl.BlockSpec((MIX_WIDTH, D_MODEL), lambda i: (0, 0)), vec, vec, vec],
        out_specs=[full] * len(out_dtypes),
        out_shape=[jax.ShapeDtypeStruct((n, D_MODEL), dt) for dt in out_dtypes],
        compiler_params=_params(("parallel",)),
        name="out_proj_norm",
    )(ya.reshape(n, WIDTH_A), yb.reshape(n, WIDTH_B), gate, h, *ln_args, w_bf16,
      as_vec(b_out), as_vec(g), as_vec(beta))


def kernel(x, ln0_g, ln0_b, w_in, w_out, b_out, ln_g, ln_b, rel_bias):
    b, t, _ = x.shape
    n = b * t
    bias_tabs = _bias_tables(rel_bias)
    seq = lambda a: a.reshape(b, t, a.shape[-1])
    h, hb = x.reshape(n, D_MODEL), None
    for l in range(DEPTH):
        ln0 = (ln0_g, ln0_b) if l == 0 else None
        (qa, ka, va), (qb, kb, vb, gate) = _project(h if l == 0 else hb, w_in[l].astype(BF16), b, ln0)
        ya = _dilated(qa, ka, va, bias_tabs)
        yb = _stick_breaking(seq(qb), seq(kb), seq(vb))
        outs = _out_block(ya, yb, gate, h, w_out[l].astype(BF16), b_out[l], ln_g[l], ln_b[l],
                          ln0=ln0, emit_bf16=l < DEPTH - 1)
        h, hb = outs[0], (outs[1] if l < DEPTH - 1 else None)
    return h.reshape(b, t, D_MODEL)
```

```python
import functools
import math

import numpy as np
import jax
import jax.numpy as jnp
from jax import lax
from jax.experimental import pallas as pl
from jax.experimental.pallas import tpu as pltpu

D_MODEL = 1024
DEPTH = 2
HEAD_DIM = 64
N_HEADS_A = 8
N_HEADS_B = 8
WIDTH_A = N_HEADS_A * HEAD_DIM
WIDTH_B = N_HEADS_B * HEAD_DIM
MIX_WIDTH = WIDTH_A + WIDTH_B
IN_COLS = 3 * WIDTH_A + 3 * WIDTH_B + MIX_WIDTH
DILATED_CONFIGS = ((128, 1), (512, 4), (2048, 16))
N_CFG = len(DILATED_CONFIGS)
BLK = 128
N_BUCKETS = 32
MAX_DISTANCE = 2048
LN_EPS = 1e-5
DEEPNORM_ALPHA = (2.0 * DEPTH) ** 0.25
QK_SCALE = 1.0 / math.sqrt(HEAD_DIM)
LOG2E = math.log2(math.e)

LANES = 128
PAIRS_A = WIDTH_A // LANES
PAIRS_B = WIDTH_B // LANES
NEG_BIG = -1e30

ROW_TILE = 512
OUT_ROW_TILE = 1024
SB_BLOCK = 256
SB_SUBS = 8
SB_DEAD = -110.0
DIL_CHUNK = BLK * max(d for _, d in DILATED_CONFIGS)
DIL_BLOCKS = DIL_CHUNK // BLK
VMEM_LIMIT = 48 * 1024 * 1024

F32 = jnp.float32
BF16 = jnp.bfloat16


def _params(semantics):
    return pltpu.CompilerParams(dimension_semantics=semantics, vmem_limit_bytes=VMEM_LIMIT)


def _layer_norm_rows(x, g, b):
    mu = jnp.mean(x, axis=-1, keepdims=True)
    xc = x - mu
    var = jnp.mean(xc * xc, axis=-1, keepdims=True)
    return xc * lax.rsqrt(var + LN_EPS) * g + b


_PROJ_A = ((0, QK_SCALE * LOG2E), (WIDTH_A, 1.0), (2 * WIDTH_A, 1.0))
_PROJ_B = ((3 * WIDTH_A, QK_SCALE), (3 * WIDTH_A + WIDTH_B, 1.0), (3 * WIDTH_A + 2 * WIDTH_B, 1.0))
_GATE_COL = 3 * WIDTH_A + 3 * WIDTH_B
_DILS = tuple(d for _, d in DILATED_CONFIGS)


def _proj_kernel(*refs, normalize):
    if normalize:
        x_ref, g_ref, b_ref, w_ref, *refs = refs
        h = _layer_norm_rows(x_ref[...], g_ref[...], b_ref[...]).astype(BF16)
    else:
        h_ref, w_ref, *refs = refs
        h = h_ref[...]
    a_refs, (b_ref, gate_ref), fold_refs = refs[:N_CFG], refs[N_CFG:N_CFG + 2], refs[N_CFG + 2:]
    slabs = range(WIDTH_A // LANES)
    for t, (c0, scale) in enumerate(_PROJ_A):
        cols = slice(t * WIDTH_A, (t + 1) * WIDTH_A)
        acc = jnp.dot(h, w_ref[:, c0:c0 + WIDTH_A], preferred_element_type=F32)
        if scale != 1.0:
            acc = acc * scale
        a_refs[0][0, :, cols] = acc.astype(BF16)
        src_ref, src_dil = fold_refs[2 * t], 1
        for j in slabs:
            src_ref[j] = acc[:, j * LANES:(j + 1) * LANES]
        for cfg in range(1, N_CFG):
            dil, dst_ref = _DILS[cfg], fold_refs[2 * t + cfg % 2]
            step, rows_out = dil // src_dil, ROW_TILE // dil
            for r_src in range(src_dil):
                for a in range(step):
                    r = r_src + src_dil * a
                    rows = pl.ds(r_src * (ROW_TILE // src_dil) + a, rows_out, stride=step)
                    parts = [src_ref[j, rows, :] for j in slabs]
                    a_refs[cfg][r, :, cols] = jnp.concatenate(parts, axis=1).astype(BF16)
                    if cfg + 1 < N_CFG:
                        for j in slabs:
                            dst_ref[j, r * rows_out:(r + 1) * rows_out, :] = parts[j]
            src_ref, src_dil = dst_ref, dil
    for t, (c0, scale) in enumerate(_PROJ_B):
        acc = jnp.dot(h, w_ref[:, c0:c0 + WIDTH_B], preferred_element_type=F32)
        if scale != 1.0:
            acc = acc * scale
        b_ref[:, t * WIDTH_B:(t + 1) * WIDTH_B] = acc.astype(BF16)
    gate_ref[...] = jnp.dot(h, w_ref[:, _GATE_COL:_GATE_COL + MIX_WIDTH], preferred_element_type=F32).astype(BF16)


def _project(rows, w_bf16, batch, ln0=None):
    n = rows.shape[0]
    t = n // batch
    tiles = t // ROW_TILE
    assert t % ROW_TILE == 0 and all(ROW_TILE % (16 * d) == 0 for d in _DILS)
    row = pl.BlockSpec((ROW_TILE, D_MODEL), lambda i: (i, 0))
    vec = pl.BlockSpec((1, D_MODEL), lambda i: (0, 0))
    weights = pl.BlockSpec((D_MODEL, IN_COLS), lambda i: (0, 0))
    ln_args = () if ln0 is None else tuple(a.reshape(1, D_MODEL) for a in ln0)
    a_specs = [pl.BlockSpec((None, d, ROW_TILE // d, 3 * WIDTH_A), lambda i: (i // tiles, 0, i % tiles, 0))
               for d in _DILS]
    a_shapes = [jax.ShapeDtypeStruct((batch, d, t // d, 3 * WIDTH_A), BF16) for d in _DILS]
    b_widths = (3 * WIDTH_B, MIX_WIDTH)
    outs = pl.pallas_call(
        functools.partial(_proj_kernel, normalize=ln0 is not None),
        grid=(n // ROW_TILE,),
        in_specs=[row] + [vec] * len(ln_args) + [weights],
        out_specs=a_specs + [pl.BlockSpec((ROW_TILE, w), lambda i: (i, 0)) for w in b_widths],
        out_shape=a_shapes + [jax.ShapeDtypeStruct((n, w), BF16) for w in b_widths],
        scratch_shapes=[pltpu.VMEM((WIDTH_A // LANES, ROW_TILE, LANES), F32)] * (2 * len(_PROJ_A)),
        compiler_params=_params(("parallel",)),
        name="in_proj",
    )(rows, *ln_args, w_bf16)
    return outs[:N_CFG], outs[N_CFG], outs[N_CFG + 1]


def _split_pair(x):
    lane = lax.broadcasted_iota(jnp.int32, x.shape, 1)
    x = x.astype(F32)
    lo = jnp.where(lane < HEAD_DIM, x, 0.0).astype(BF16)
    hi = jnp.where(lane >= HEAD_DIM, x, 0.0).astype(BF16)
    return lo, hi


def _merge_pair(x0, x1):
    lane = lax.broadcasted_iota(jnp.int32, x0.shape, 1)
    return jnp.where(lane < HEAD_DIM, x0, x1)


_NT = (((1,), (1,)), ((), ()))


def _t5_bucket(dist):
    max_exact = N_BUCKETS // 2
    n = np.maximum(dist, 1).astype(np.float32)
    large = max_exact + (np.log(n / max_exact) / math.log(MAX_DISTANCE / max_exact)
                         * (N_BUCKETS - max_exact)).astype(np.int32)
    large = np.minimum(large, N_BUCKETS - 1)
    return np.where(dist < max_exact, dist, large).astype(np.int32)


def _bucket_segments(window, dil):
    buckets = _t5_bucket(np.arange(window // dil + 1) * dil)
    return [(d, int(bk)) for d, bk in enumerate(buckets) if d == 0 or bk != buckets[d - 1]]


def _bias_kernel(rel_ref, out_ref):
    i = lax.broadcasted_iota(jnp.int32, (BLK, 2 * BLK), 0)
    j = lax.broadcasted_iota(jnp.int32, (BLK, 2 * BLK), 1)
    for cfg, (window, dil) in enumerate(DILATED_CONFIGS):
        segments = _bucket_segments(window, dil)
        for variant in range(2):
            delta = (BLK + i - j) if variant == 0 else (i - j)
            band = (delta >= 0) & (delta <= window // dil)
            steps = [delta >= start for start, _ in segments[1:]]
            for h in range(N_HEADS_A):
                val = jnp.full((BLK, 2 * BLK), rel_ref[segments[0][1], h] * LOG2E, F32)
                for step, (_, bucket) in zip(steps, segments[1:]):
                    val = jnp.where(step, rel_ref[bucket, h] * LOG2E, val)
                out_ref[cfg, variant, h] = jnp.where(band, val, NEG_BIG)


def _bias_tables(rel_bias):
    shape = (N_CFG, 2, N_HEADS_A, BLK, 2 * BLK)
    return pl.pallas_call(
        _bias_kernel,
        in_specs=[pl.BlockSpec(memory_space=pltpu.SMEM)],
        out_shape=jax.ShapeDtypeStruct(shape, F32),
        compiler_params=pltpu.CompilerParams(vmem_limit_bytes=VMEM_LIMIT),
        name="bias_tables",
    )(rel_bias.astype(F32))


def _dilated_kernel(*refs):
    q_refs, k_refs, v_refs = refs[:N_CFG], refs[N_CFG:2 * N_CFG], refs[2 * N_CFG:3 * N_CFG]
    bias_ref, ones_ref, o_ref, *state_refs = refs[3 * N_CFG:]
    c = pl.program_id(2)
    n_state = N_CFG - 1
    m_refs, den_refs, num_refs = state_refs[:n_state], state_refs[n_state:2 * n_state], state_refs[2 * n_state:]

    def attend(cfg, dil, r, n):
        per_res = DIL_BLOCKS // dil
        g = c * per_res + n
        k_rows = pl.ds(pl.multiple_of(jnp.maximum(g - 1, 0) * BLK, BLK), 2 * BLK)
        first = jnp.where(g == 0, 1, 0) if n == 0 else 0
        ones = ones_ref[...]
        q = q_refs[cfg][r, n * BLK:(n + 1) * BLK, :]
        q_pair = (q * ones[0:BLK], q * ones[2 * BLK:3 * BLK])
        kb = k_refs[cfg][r, k_rows, :]
        vb = v_refs[cfg][r, k_rows, :]
        v_sum = jnp.concatenate([jnp.concatenate([vb, vb], axis=0) * ones, ones], axis=1)
        ms, ps = [], []
        for h in range(2):
            s = lax.dot_general(q_pair[h], kb, _NT, preferred_element_type=F32) + bias_ref[cfg, first, h]
            m = jnp.max(s, axis=-1, keepdims=True)
            ps.append(jnp.exp2(s - m).astype(BF16))
            ms.append(jnp.broadcast_to(m, (BLK, LANES)))
        both = jnp.dot(jnp.concatenate(ps, axis=1), v_sum, preferred_element_type=F32)
        m, num, den = _merge_pair(*ms), both[:, :LANES], both[:, LANES:]
        if cfg > 0:
            rows = pl.ds(n * BLK * dil + r, BLK, stride=dil)
            m_refs[cfg - 1][rows, :], num_refs[cfg - 1][rows, :], den_refs[cfg - 1][rows, :] = m, num, den
            return
        rows = pl.ds(n * BLK, BLK)
        others = [ref[rows, :] for ref in m_refs]
        top = functools.reduce(jnp.maximum, others, m)
        w = jnp.exp2(m - top)
        num, den = w * num, w * den
        for m_c, num_ref, den_ref in zip(others, num_refs, den_refs):
            w = jnp.exp2(m_c - top)
            num, den = num + w * num_ref[rows, :], den + w * den_ref[rows, :]
        o_ref[rows, :] = (num / den).astype(o_ref.dtype)

    assert _DILS[0] == 1
    for cfg in reversed(range(N_CFG)):
        dil = _DILS[cfg]
        for r in range(dil):
            for n in range(DIL_BLOCKS // dil):
                attend(cfg, dil, r, n)


def _dilated(qkv, bias_tabs):
    b, _, t, _ = qkv[0].shape
    assert t % DIL_CHUNK == 0
    q_specs = [pl.BlockSpec((None, d, DIL_CHUNK // d, LANES), lambda bi, p, c: (bi, 0, c, p)) for d in _DILS]
    k_specs = [pl.BlockSpec((None, d, t // d, LANES), lambda bi, p, c: (bi, 0, 0, PAIRS_A + p)) for d in _DILS]
    v_specs = [pl.BlockSpec((None, d, t // d, LANES), lambda bi, p, c: (bi, 0, 0, 2 * PAIRS_A + p)) for d in _DILS]
    bias_spec = pl.BlockSpec((N_CFG, 2, 2, BLK, 2 * BLK), lambda bi, p, c: (0, 0, p, 0, 0))
    head_of_lane = np.arange(LANES) // HEAD_DIM
    head_of_row = np.arange(4 * BLK) // (2 * BLK)
    ones = jnp.asarray(head_of_row[:, None] == head_of_lane[None, :], dtype=BF16)
    return pl.pallas_call(
        _dilated_kernel,
        grid=(b, PAIRS_A, t // DIL_CHUNK),
        in_specs=q_specs + k_specs + v_specs
                 + [bias_spec, pl.BlockSpec((4 * BLK, LANES), lambda bi, p, c: (0, 0))],
        out_specs=pl.BlockSpec((None, DIL_CHUNK, LANES), lambda bi, p, c: (bi, c, p)),
        out_shape=jax.ShapeDtypeStruct((b, t, WIDTH_A), BF16),
        scratch_shapes=[pltpu.VMEM((DIL_CHUNK, LANES), F32)] * (3 * (N_CFG - 1)),
        compiler_params=_params(("parallel", "parallel", "arbitrary")),
        name="dilated",
    )(*qkv, *qkv, *qkv, bias_tabs, ones)


def _softplus(z):
    return jnp.maximum(z, 0.0) + jnp.log(1.0 + jnp.exp2(jnp.abs(z) * (-LOG2E)))


def _sb_kernel(q_ref, k_ref, v_ref, tri_ref, o_ref, qs_ref, acc_ref, carry_ref, live_ref):
    step_id = pl.program_id(2)
    bq = SB_BLOCK
    for s in range(SB_SUBS):
        q0, q1 = _split_pair(q_ref[s * bq:(s + 1) * bq, :])
        qs_ref[s] = jnp.concatenate([q0, q1], axis=0)
    tri = tri_ref[...]

    def step(qs, j, carry):
        diagonal = carry is None
        ks = pl.multiple_of(j * bq, bq)
        kb = k_ref[pl.ds(ks, bq), :]
        vb = v_ref[pl.ds(ks, bq), :]
        z = lax.dot_general(qs, kb, _NT, preferred_element_type=F32)
        t = _softplus(z)
        if diagonal:
            row = lax.broadcasted_iota(jnp.int32, z.shape, 0) & (bq - 1)
            colv = lax.broadcasted_iota(jnp.int32, z.shape, 1)
            causal = colv < row
            t = jnp.where(causal, t, 0.0)
        after = jnp.dot(t.astype(BF16), tri, preferred_element_type=F32)
        expo = (z - t) + after
        if not diagonal:
            expo = expo + jnp.concatenate([carry] * (bq // LANES), axis=1)
        a = jnp.exp2(expo * LOG2E)
        if diagonal:
            a = jnp.where(causal, a, 0.0)
        pv = jnp.dot(a.astype(BF16), vb, preferred_element_type=F32)
        block_sum = jnp.broadcast_to(after[:, 0:1] - t[:, 0:1], (2 * bq, LANES))
        return pv, block_sum

    def run(s, first_block, count, fresh):
        qs = qs_ref[s]
        acc, carry = (None, None) if fresh else (acc_ref[s], carry_ref[s])
        for u in range(count):
            pv, block_sum = step(qs, first_block - u, carry)
            acc = pv if acc is None else acc + pv
            carry = block_sum if carry is None else carry + block_sum
        acc_ref[s], carry_ref[s] = acc, carry
        return jnp.max(carry)

    @pl.when(step_id == 0)
    def _():
        run(0, 0, 1, True)
        live_ref[0] = SB_DEAD
        for s in range(1, SB_SUBS):
            live_ref[s] = run(s, s, 2, True)

    @pl.when(step_id > 0)
    def _():
        for s in range(SB_SUBS):
            live_ref[s] = run(s, step_id * SB_SUBS + s, 2, True)

    def older(s, c):
        def more(state):
            j, live = state
            return (j >= 1) & (live > SB_DEAD)

        def pair(state):
            j, _ = state
            return j - 2, run(s, j, 2, False)

        j, live = lax.while_loop(more, pair, (step_id * SB_SUBS + s - 2, live_ref[s]))

        @pl.when((j == 0) & (live > SB_DEAD))
        def _():
            run(s, 0, 1, False)

        return c

    lax.fori_loop(0, SB_SUBS, older, 0)
    for s in range(SB_SUBS):
        o_ref[s * bq:(s + 1) * bq, :] = _merge_pair(acc_ref[s, 0:bq, :], acc_ref[s, bq:2 * bq, :]).astype(o_ref.dtype)


def _stick_breaking(qkv):
    b, t, _ = qkv.shape
    bq = SB_BLOCK
    rows = SB_SUBS * bq
    assert t % rows == 0
    r = np.arange(bq)
    tri = jnp.asarray(np.where(r[:, None] > r[None, :], -1.0, 0.0), dtype=BF16)
    q_spec = pl.BlockSpec((None, rows, LANES), lambda bi, p, i: (bi, i, p))
    k_spec = pl.BlockSpec((None, t, LANES), lambda bi, p, i: (bi, 0, PAIRS_B + p))
    v_spec = pl.BlockSpec((None, t, LANES), lambda bi, p, i: (bi, 0, 2 * PAIRS_B + p))
    return pl.pallas_call(
        _sb_kernel,
        grid=(b, PAIRS_B, t // rows),
        in_specs=[q_spec, k_spec, v_spec, pl.BlockSpec((bq, bq), lambda bi, p, i: (0, 0))],
        out_specs=q_spec,
        out_shape=jax.ShapeDtypeStruct((b, t, WIDTH_B), BF16),
        scratch_shapes=[pltpu.VMEM((SB_SUBS, 2 * bq, LANES), BF16), pltpu.VMEM((SB_SUBS, 2 * bq, LANES), F32),
                        pltpu.VMEM((SB_SUBS, 2 * bq, LANES), F32), pltpu.SMEM((SB_SUBS,), F32)],
        compiler_params=_params(("parallel", "parallel", "arbitrary")),
        name="stick_breaking",
    )(qkv, qkv, qkv, tri)


def _out_kernel(*refs, normalize, emit_bf16):
    ya_ref, yb_ref, gate_ref, h_ref, *refs = refs
    h = h_ref[...]
    if normalize:
        g0_ref, b0_ref, *refs = refs
        h = _layer_norm_rows(h, g0_ref[...], b0_ref[...])
    w_ref, b_ref, g_ref, beta_ref, hn_ref, *hnb_ref = refs
    half_gate = 0.5 * gate_ref[...].astype(F32)
    silu = half_gate + half_gate * jnp.tanh(half_gate)
    ya = (ya_ref[...].astype(F32) * silu[:, :WIDTH_A]).astype(BF16)
    yb = (yb_ref[...].astype(F32) * silu[:, WIDTH_A:]).astype(BF16)
    out = (jnp.dot(ya, w_ref[0:WIDTH_A, :], preferred_element_type=F32)
           + jnp.dot(yb, w_ref[WIDTH_A:MIX_WIDTH, :], preferred_element_type=F32)
           + b_ref[...])
    y = _layer_norm_rows(DEEPNORM_ALPHA * h + out, g_ref[...], beta_ref[...])
    hn_ref[...] = y
    if emit_bf16:
        hnb_ref[0][...] = y.astype(BF16)


def _out_block(ya, yb, gate, h, w_bf16, b_out, g, beta, ln0=None, emit_bf16=True):
    n = h.shape[0]
    half = pl.BlockSpec((OUT_ROW_TILE, WIDTH_A), lambda i: (i, 0))
    full = pl.BlockSpec((OUT_ROW_TILE, D_MODEL), lambda i: (i, 0))
    vec = pl.BlockSpec((1, D_MODEL), lambda i: (0, 0))
    as_vec = lambda a: a.reshape(1, D_MODEL)
    ln_args = () if ln0 is None else tuple(as_vec(a) for a in ln0)
    out_dtypes = (F32, BF16) if emit_bf16 else (F32,)
    return pl.pallas_call(
        functools.partial(_out_kernel, normalize=ln0 is not None, emit_bf16=emit_bf16),
        grid=(n // OUT_ROW_TILE,),
        in_specs=[half, half, full, full] + [vec] * len(ln_args)
                 + [pl.BlockSpec((MIX_WIDTH, D_MODEL), lambda i: (0, 0)), vec, vec, vec],
        out_specs=[full] * len(out_dtypes),
        out_shape=[jax.ShapeDtypeStruct((n, D_MODEL), dt) for dt in out_dtypes],
        compiler_params=_params(("parallel",)),
        name="out_proj_norm",
    )(ya.reshape(n, WIDTH_A), yb.reshape(n, WIDTH_B), gate, h, *ln_args, w_bf16,
      as_vec(b_out), as_vec(g), as_vec(beta))


def kernel(x, ln0_g, ln0_b, w_in, w_out, b_out, ln_g, ln_b, rel_bias):
    b, t, _ = x.shape
    n = b * t
    bias_tabs = _bias_tables(rel_bias)
    h, hb = x.reshape(n, D_MODEL), None
    for l in range(DEPTH):
        ln0 = (ln0_g, ln0_b) if l == 0 else None
        qkv_a, qkv_b, gate = _project(h if l == 0 else hb, w_in[l].astype(BF16), b, ln0)
        ya = _dilated(qkv_a, bias_tabs)
        yb = _stick_breaking(qkv_b.reshape(b, t, 3 * WIDTH_B))
        outs = _out_block(ya, yb, gate, h, w_out[l].astype(BF16), b_out[l], ln_g[l], ln_b[l],
                          ln0=ln0, emit_bf16=l < DEPTH - 1)
        h, hb = outs[0], (outs[1] if l < DEPTH - 1 else None)
    return h.reshape(b, t, D_MODEL)
```

```python
import functools
import math

import numpy as np
import jax
import jax.numpy as jnp
from jax import lax
from jax.experimental import pallas as pl
from jax.experimental.pallas import tpu as pltpu

D_MODEL = 1024
DEPTH = 2
HEAD_DIM = 64
N_HEADS_A = 8
N_HEADS_B = 8
WIDTH_A = N_HEADS_A * HEAD_DIM
WIDTH_B = N_HEADS_B * HEAD_DIM
MIX_WIDTH = WIDTH_A + WIDTH_B
IN_COLS = 3 * WIDTH_A + 3 * WIDTH_B + MIX_WIDTH
DILATED_CONFIGS = ((128, 1), (512, 4), (2048, 16))
N_CFG = len(DILATED_CONFIGS)
BLK = 128
N_BUCKETS = 32
MAX_DISTANCE = 2048
LN_EPS = 1e-5
DEEPNORM_ALPHA = (2.0 * DEPTH) ** 0.25
QK_SCALE = 1.0 / math.sqrt(HEAD_DIM)
LOG2E = math.log2(math.e)

LANES = 128
PAIRS_A = WIDTH_A // LANES
PAIRS_B = WIDTH_B // LANES
NEG_BIG = -1e30

ROW_TILE = 512
OUT_ROW_TILE = 1024
SB_BLOCK = 256
SB_SUBS = 8
SB_DEAD = -160.0
DIL_CHUNK = BLK * max(d for _, d in DILATED_CONFIGS)
DIL_BLOCKS = DIL_CHUNK // BLK
VMEM_LIMIT = 48 * 1024 * 1024

F32 = jnp.float32
BF16 = jnp.bfloat16


def _params(semantics):
    return pltpu.CompilerParams(dimension_semantics=semantics, vmem_limit_bytes=VMEM_LIMIT)


def _layer_norm_rows(x, g, b):
    mu = jnp.mean(x, axis=-1, keepdims=True)
    xc = x - mu
    var = jnp.mean(xc * xc, axis=-1, keepdims=True)
    return xc * lax.rsqrt(var + LN_EPS) * g + b


_PROJ_A = ((0, QK_SCALE * LOG2E), (WIDTH_A, 1.0), (2 * WIDTH_A, 1.0))
_PROJ_B = ((3 * WIDTH_A, QK_SCALE * LOG2E), (3 * WIDTH_A + WIDTH_B, 1.0), (3 * WIDTH_A + 2 * WIDTH_B, 1.0))
_GATE_COL = 3 * WIDTH_A + 3 * WIDTH_B
_DILS = tuple(d for _, d in DILATED_CONFIGS)


def _proj_kernel(*refs, normalize):
    if normalize:
        x_ref, g_ref, b_ref, w_ref, *refs = refs
        h = _layer_norm_rows(x_ref[...], g_ref[...], b_ref[...]).astype(BF16)
    else:
        h_ref, w_ref, *refs = refs
        h = h_ref[...]
    a_refs, (b_ref, gate_ref), fold_refs = refs[:N_CFG], refs[N_CFG:N_CFG + 2], refs[N_CFG + 2:]
    slabs = range(WIDTH_A // LANES)
    for t, (c0, scale) in enumerate(_PROJ_A):
        cols = slice(t * WIDTH_A, (t + 1) * WIDTH_A)
        acc = jnp.dot(h, w_ref[:, c0:c0 + WIDTH_A], preferred_element_type=F32)
        if scale != 1.0:
            acc = acc * scale
        a_refs[0][0, :, cols] = acc.astype(BF16)
        src_ref, src_dil = fold_refs[2 * t], 1
        for j in slabs:
            src_ref[j] = acc[:, j * LANES:(j + 1) * LANES]
        for cfg in range(1, N_CFG):
            dil, dst_ref = _DILS[cfg], fold_refs[2 * t + cfg % 2]
            step, rows_out = dil // src_dil, ROW_TILE // dil
            for r_src in range(src_dil):
                for a in range(step):
                    r = r_src + src_dil * a
                    rows = pl.ds(r_src * (ROW_TILE // src_dil) + a, rows_out, stride=step)
                    parts = [src_ref[j, rows, :] for j in slabs]
                    a_refs[cfg][r, :, cols] = jnp.concatenate(parts, axis=1).astype(BF16)
                    if cfg + 1 < N_CFG:
                        for j in slabs:
                            dst_ref[j, r * rows_out:(r + 1) * rows_out, :] = parts[j]
            src_ref, src_dil = dst_ref, dil
    for t, (c0, scale) in enumerate(_PROJ_B):
        acc = jnp.dot(h, w_ref[:, c0:c0 + WIDTH_B], preferred_element_type=F32)
        if scale != 1.0:
            acc = acc * scale
        b_ref[:, t * WIDTH_B:(t + 1) * WIDTH_B] = acc.astype(BF16)
    gate = jnp.dot(h, w_ref[:, _GATE_COL:_GATE_COL + MIX_WIDTH], preferred_element_type=F32)
    gate_ref[...] = (0.5 * gate).astype(BF16)


def _project(rows, w_bf16, layer, batch, ln0=None):
    n = rows.shape[0]
    t = n // batch
    tiles = t // ROW_TILE
    assert t % ROW_TILE == 0 and all(ROW_TILE % (16 * d) == 0 for d in _DILS)
    row = pl.BlockSpec((ROW_TILE, D_MODEL), lambda i: (i, 0))
    vec = pl.BlockSpec((1, D_MODEL), lambda i: (0, 0))
    weights = pl.BlockSpec((None, D_MODEL, IN_COLS), lambda i: (layer, 0, 0))
    ln_args = () if ln0 is None else tuple(a.reshape(1, D_MODEL) for a in ln0)
    a_specs = [pl.BlockSpec((None, d, ROW_TILE // d, 3 * WIDTH_A), lambda i: (i // tiles, 0, i % tiles, 0))
               for d in _DILS]
    a_shapes = [jax.ShapeDtypeStruct((batch, d, t // d, 3 * WIDTH_A), BF16) for d in _DILS]
    b_widths = (3 * WIDTH_B, MIX_WIDTH)
    outs = pl.pallas_call(
        functools.partial(_proj_kernel, normalize=ln0 is not None),
        grid=(n // ROW_TILE,),
        in_specs=[row] + [vec] * len(ln_args) + [weights],
        out_specs=a_specs + [pl.BlockSpec((ROW_TILE, w), lambda i: (i, 0)) for w in b_widths],
        out_shape=a_shapes + [jax.ShapeDtypeStruct((n, w), BF16) for w in b_widths],
        scratch_shapes=[pltpu.VMEM((WIDTH_A // LANES, ROW_TILE, LANES), F32)] * (2 * len(_PROJ_A)),
        compiler_params=_params(("parallel",)),
        name="in_proj",
    )(rows, *ln_args, w_bf16)
    return outs[:N_CFG], outs[N_CFG], outs[N_CFG + 1]


def _head_rows(rows_per_head):
    head_of_lane = np.arange(LANES) // HEAD_DIM
    head_of_row = np.arange(2 * rows_per_head) // rows_per_head
    return jnp.asarray(head_of_row[:, None] == head_of_lane[None, :], dtype=BF16)


def _merge_pair(x0, x1):
    lane = lax.broadcasted_iota(jnp.int32, x0.shape, 1)
    return jnp.where(lane < HEAD_DIM, x0, x1)


_NT = (((1,), (1,)), ((), ()))


def _t5_bucket(dist):
    max_exact = N_BUCKETS // 2
    n = np.maximum(dist, 1).astype(np.float32)
    large = max_exact + (np.log(n / max_exact) / math.log(MAX_DISTANCE / max_exact)
                         * (N_BUCKETS - max_exact)).astype(np.int32)
    large = np.minimum(large, N_BUCKETS - 1)
    return np.where(dist < max_exact, dist, large).astype(np.int32)


def _bucket_segments(window, dil):
    buckets = _t5_bucket(np.arange(window // dil + 1) * dil)
    return [(d, int(bk)) for d, bk in enumerate(buckets) if d == 0 or bk != buckets[d - 1]]


def _bias_kernel(rel_ref, out_ref):
    i = lax.broadcasted_iota(jnp.int32, (BLK, 2 * BLK), 0)
    j = lax.broadcasted_iota(jnp.int32, (BLK, 2 * BLK), 1)
    for cfg, (window, dil) in enumerate(DILATED_CONFIGS):
        segments = _bucket_segments(window, dil)
        for variant in range(2):
            delta = (BLK + i - j) if variant == 0 else (i - j)
            band = (delta >= 0) & (delta <= window // dil)
            steps = [delta >= start for start, _ in segments[1:]]
            for h in range(N_HEADS_A):
                val = jnp.full((BLK, 2 * BLK), rel_ref[segments[0][1], h] * LOG2E, F32)
                for step, (_, bucket) in zip(steps, segments[1:]):
                    val = jnp.where(step, rel_ref[bucket, h] * LOG2E, val)
                out_ref[cfg, variant, h] = jnp.where(band, val, NEG_BIG)


def _bias_tables(rel_bias):
    shape = (N_CFG, 2, N_HEADS_A, BLK, 2 * BLK)
    return pl.pallas_call(
        _bias_kernel,
        in_specs=[pl.BlockSpec(memory_space=pltpu.SMEM)],
        out_shape=jax.ShapeDtypeStruct(shape, F32),
        compiler_params=pltpu.CompilerParams(vmem_limit_bytes=VMEM_LIMIT),
        name="bias_tables",
    )(rel_bias.astype(F32))


def _dilated_kernel(*refs):
    q_refs, k_refs, v_refs = refs[:N_CFG], refs[N_CFG:2 * N_CFG], refs[2 * N_CFG:3 * N_CFG]
    bias_ref, ones_ref, o_ref, *state_refs = refs[3 * N_CFG:]
    c = pl.program_id(2)
    n_state = N_CFG - 1
    m_refs, den_refs, num_refs = state_refs[:n_state], state_refs[n_state:2 * n_state], state_refs[2 * n_state:]

    def attend(cfg, dil, r, n):
        per_res = DIL_BLOCKS // dil
        g = c * per_res + n
        k_rows = pl.ds(pl.multiple_of(jnp.maximum(g - 1, 0) * BLK, BLK), 2 * BLK)
        first = jnp.where(g == 0, 1, 0) if n == 0 else 0
        ones = ones_ref[...]
        q = q_refs[cfg][r, n * BLK:(n + 1) * BLK, :]
        q_pair = (q * ones[0:BLK], q * ones[2 * BLK:3 * BLK])
        kb = k_refs[cfg][r, k_rows, :]
        vb = v_refs[cfg][r, k_rows, :]
        v_sum = jnp.concatenate([jnp.concatenate([vb, vb], axis=0) * ones, ones], axis=1)
        ms, ps = [], []
        for h in range(2):
            s = lax.dot_general(q_pair[h], kb, _NT, preferred_element_type=F32) + bias_ref[cfg, first, h]
            m = jnp.max(s, axis=-1, keepdims=True)
            ps.append(jnp.exp2(s - m).astype(BF16))
            ms.append(jnp.broadcast_to(m, (BLK, LANES)))
        both = jnp.dot(jnp.concatenate(ps, axis=1), v_sum, preferred_element_type=F32)
        m, num, den = _merge_pair(*ms), both[:, :LANES], both[:, LANES:]
        if cfg > 0:
            rows = pl.ds(n * BLK * dil + r, BLK, stride=dil)
            m_refs[cfg - 1][rows, :], num_refs[cfg - 1][rows, :], den_refs[cfg - 1][rows, :] = m, num, den
            return
        rows = pl.ds(n * BLK, BLK)
        others = [ref[rows, :] for ref in m_refs]
        top = functools.reduce(jnp.maximum, others, m)
        w = jnp.exp2(m - top)
        num, den = w * num, w * den
        for m_c, num_ref, den_ref in zip(others, num_refs, den_refs):
            w = jnp.exp2(m_c - top)
            num, den = num + w * num_ref[rows, :], den + w * den_ref[rows, :]
        o_ref[rows, :] = (num / den).astype(o_ref.dtype)

    assert _DILS[0] == 1
    for cfg in reversed(range(N_CFG)):
        dil = _DILS[cfg]
        for r in range(dil):
            for n in range(DIL_BLOCKS // dil):
                attend(cfg, dil, r, n)


def _dilated(qkv, bias_tabs):
    b, _, t, _ = qkv[0].shape
    assert t % DIL_CHUNK == 0
    q_specs = [pl.BlockSpec((None, d, DIL_CHUNK // d, LANES), lambda bi, p, c: (bi, 0, c, p)) for d in _DILS]
    k_specs = [pl.BlockSpec((None, d, t // d, LANES), lambda bi, p, c: (bi, 0, 0, PAIRS_A + p)) for d in _DILS]
    v_specs = [pl.BlockSpec((None, d, t // d, LANES), lambda bi, p, c: (bi, 0, 0, 2 * PAIRS_A + p)) for d in _DILS]
    bias_spec = pl.BlockSpec((N_CFG, 2, 2, BLK, 2 * BLK), lambda bi, p, c: (0, 0, p, 0, 0))
    ones = _head_rows(2 * BLK)
    return pl.pallas_call(
        _dilated_kernel,
        grid=(b, PAIRS_A, t // DIL_CHUNK),
        in_specs=q_specs + k_specs + v_specs
                 + [bias_spec, pl.BlockSpec((4 * BLK, LANES), lambda bi, p, c: (0, 0))],
        out_specs=pl.BlockSpec((None, DIL_CHUNK, LANES), lambda bi, p, c: (bi, c, p)),
        out_shape=jax.ShapeDtypeStruct((b, t, WIDTH_A), BF16),
        scratch_shapes=[pltpu.VMEM((DIL_CHUNK, LANES), F32)] * (3 * (N_CFG - 1)),
        compiler_params=_params(("parallel", "parallel", "arbitrary")),
        name="dilated",
    )(*qkv, *qkv, *qkv, bias_tabs, ones)


def _softplus2(z):
    minus_abs = lax.bitcast_convert_type(lax.bitcast_convert_type(z, jnp.int32) | jnp.int32(-2 ** 31), F32)
    return jnp.maximum(z, 0.0) + jnp.log(1.0 + jnp.exp2(minus_abs)) * LOG2E


def _sb_kernel(q_ref, k_ref, v_ref, tri_ref, heads_ref, o_ref, qs_ref, acc_ref, carry_ref, live_ref):
    step_id = pl.program_id(2)
    bq = SB_BLOCK
    for s in range(SB_SUBS):
        q = q_ref[s * bq:(s + 1) * bq, :]
        qs_ref[s] = jnp.concatenate([q, q], axis=0) * heads_ref[...]
    tri = tri_ref[...]

    def step(qs, j, carry):
        diagonal = carry is None
        ks = pl.multiple_of(j * bq, bq)
        kb = k_ref[pl.ds(ks, bq), :]
        vb = v_ref[pl.ds(ks, bq), :]
        z = lax.dot_general(qs, kb, _NT, preferred_element_type=F32)
        t = _softplus2(z)
        if diagonal:
            row = lax.broadcasted_iota(jnp.int32, z.shape, 0) & (bq - 1)
            colv = lax.broadcasted_iota(jnp.int32, z.shape, 1)
            causal = colv < row
            t = jnp.where(causal, t, 0.0)
        after = jnp.dot(t.astype(BF16), tri, preferred_element_type=F32)
        expo = (z - t) + after
        if not diagonal:
            expo = expo + jnp.concatenate([carry] * (bq // LANES), axis=1)
        a = jnp.exp2(expo)
        if diagonal:
            a = jnp.where(causal, a, 0.0)
        pv = jnp.dot(a.astype(BF16), vb, preferred_element_type=F32)
        block_sum = jnp.broadcast_to(after[:, 0:1] - t[:, 0:1], (2 * bq, LANES))
        return pv, block_sum

    def run(s, first_block, count, fresh):
        qs = qs_ref[s]
        acc, carry = (None, None) if fresh else (acc_ref[s], carry_ref[s])
        for u in range(count):
            pv, block_sum = step(qs, first_block - u, carry)
            acc = pv if acc is None else acc + pv
            carry = block_sum if carry is None else carry + block_sum
        acc_ref[s], carry_ref[s] = acc, carry
        return jnp.max(carry)

    @pl.when(step_id == 0)
    def _():
        run(0, 0, 1, True)
        live_ref[0] = SB_DEAD
        for s in range(1, SB_SUBS):
            live_ref[s] = run(s, s, 2, True)

    @pl.when(step_id > 0)
    def _():
        for s in range(SB_SUBS):
            live_ref[s] = run(s, step_id * SB_SUBS + s, 2, True)

    def older(s, c):
        def more(state):
            j, live = state
            return (j >= 1) & (live > SB_DEAD)

        def pair(state):
            j, _ = state
            return j - 2, run(s, j, 2, False)

        j, live = lax.while_loop(more, pair, (step_id * SB_SUBS + s - 2, live_ref[s]))

        @pl.when((j == 0) & (live > SB_DEAD))
        def _():
            run(s, 0, 1, False)

        return c

    lax.fori_loop(0, SB_SUBS, older, 0)
    for s in range(SB_SUBS):
        o_ref[s * bq:(s + 1) * bq, :] = _merge_pair(acc_ref[s, 0:bq, :], acc_ref[s, bq:2 * bq, :]).astype(o_ref.dtype)


def _stick_breaking(qkv):
    b, t, _ = qkv.shape
    bq = SB_BLOCK
    rows = SB_SUBS * bq
    assert t % rows == 0
    r = np.arange(bq)
    tri = jnp.asarray(np.where(r[:, None] > r[None, :], -1.0, 0.0), dtype=BF16)
    heads = _head_rows(bq)
    q_spec = pl.BlockSpec((None, rows, LANES), lambda bi, p, i: (bi, i, p))
    k_spec = pl.BlockSpec((None, t, LANES), lambda bi, p, i: (bi, 0, PAIRS_B + p))
    v_spec = pl.BlockSpec((None, t, LANES), lambda bi, p, i: (bi, 0, 2 * PAIRS_B + p))
    const = lambda shape: pl.BlockSpec(shape, lambda bi, p, i: (0, 0))
    return pl.pallas_call(
        _sb_kernel,
        grid=(b, PAIRS_B, t // rows),
        in_specs=[q_spec, k_spec, v_spec, const((bq, bq)), const((2 * bq, LANES))],
        out_specs=q_spec,
        out_shape=jax.ShapeDtypeStruct((b, t, WIDTH_B), BF16),
        scratch_shapes=[pltpu.VMEM((SB_SUBS, 2 * bq, LANES), BF16), pltpu.VMEM((SB_SUBS, 2 * bq, LANES), F32),
                        pltpu.VMEM((SB_SUBS, 2 * bq, LANES), F32), pltpu.SMEM((SB_SUBS,), F32)],
        compiler_params=_params(("parallel", "parallel", "arbitrary")),
        name="stick_breaking",
    )(qkv, qkv, qkv, tri, heads)


def _out_kernel(*refs, normalize, emit_bf16):
    ya_ref, yb_ref, gate_ref, h_ref, *refs = refs
    h = h_ref[...]
    if normalize:
        g0_ref, b0_ref, *refs = refs
        h = _layer_norm_rows(h, g0_ref[...], b0_ref[...])
    w_ref, b_ref, g_ref, beta_ref, hn_ref, *hnb_ref = refs
    half_gate = gate_ref[...].astype(F32)
    silu = half_gate + half_gate * jnp.tanh(half_gate)
    ya = (ya_ref[...].astype(F32) * silu[:, :WIDTH_A]).astype(BF16)
    yb = (yb_ref[...].astype(F32) * silu[:, WIDTH_A:]).astype(BF16)
    out = (jnp.dot(ya, w_ref[0:WIDTH_A, :], preferred_element_type=F32)
           + jnp.dot(yb, w_ref[WIDTH_A:MIX_WIDTH, :], preferred_element_type=F32)
           + b_ref[...])
    y = _layer_norm_rows(DEEPNORM_ALPHA * h + out, g_ref[...], beta_ref[...])
    hn_ref[...] = y
    if emit_bf16:
        hnb_ref[0][...] = y.astype(BF16)


def _out_block(ya, yb, gate, h, w_bf16, layer, b_out, g, beta, ln0=None, emit_bf16=True):
    n = h.shape[0]
    half = pl.BlockSpec((OUT_ROW_TILE, WIDTH_A), lambda i: (i, 0))
    full = pl.BlockSpec((OUT_ROW_TILE, D_MODEL), lambda i: (i, 0))
    vec = pl.BlockSpec((1, D_MODEL), lambda i: (0, 0))
    as_vec = lambda a: a.reshape(1, D_MODEL)
    ln_args = () if ln0 is None else tuple(as_vec(a) for a in ln0)
    out_dtypes = (F32, BF16) if emit_bf16 else (F32,)
    return pl.pallas_call(
        functools.partial(_out_kernel, normalize=ln0 is not None, emit_bf16=emit_bf16),
        grid=(n // OUT_ROW_TILE,),
        in_specs=[half, half, full, full] + [vec] * len(ln_args)
                 + [pl.BlockSpec((None, MIX_WIDTH, D_MODEL), lambda i: (layer, 0, 0)), vec, vec, vec],
        out_specs=[full] * len(out_dtypes),
        out_shape=[jax.ShapeDtypeStruct((n, D_MODEL), dt) for dt in out_dtypes],
        compiler_params=_params(("parallel",)),
        name="out_proj_norm",
    )(ya.reshape(n, WIDTH_A), yb.reshape(n, WIDTH_B), gate, h, *ln_args, w_bf16,
      as_vec(b_out), as_vec(g), as_vec(beta))


def kernel(x, ln0_g, ln0_b, w_in, w_out, b_out, ln_g, ln_b, rel_bias):
    b, t, _ = x.shape
    n = b * t
    bias_tabs = _bias_tables(rel_bias)
    w_in, w_out = w_in.astype(BF16), w_out.astype(BF16)
    h, hb = x.reshape(n, D_MODEL), None
    for l in range(DEPTH):
        ln0 = (ln0_g, ln0_b) if l == 0 else None
        qkv_a, qkv_b, gate = _project(h if l == 0 else hb, w_in, l, b, ln0)
        ya = _dilated(qkv_a, bias_tabs)
        yb = _stick_breaking(qkv_b.reshape(b, t, 3 * WIDTH_B))
        outs = _out_block(ya, yb, gate, h, w_out, l, b_out[l], ln_g[l], ln_b[l],
                          ln0=ln0, emit_bf16=l < DEPTH - 1)
        h, hb = outs[0], (outs[1] if l < DEPTH - 1 else None)
    return h.reshape(b, t, D_MODEL)
```

```python
import functools
import math

import numpy as np
import jax
import jax.numpy as jnp
from jax import lax
from jax.experimental import pallas as pl
from jax.experimental.pallas import tpu as pltpu

D_MODEL = 1024
DEPTH = 2
HEAD_DIM = 64
N_HEADS_A = 8
N_HEADS_B = 8
WIDTH_A = N_HEADS_A * HEAD_DIM
WIDTH_B = N_HEADS_B * HEAD_DIM
MIX_WIDTH = WIDTH_A + WIDTH_B
IN_COLS = 3 * WIDTH_A + 3 * WIDTH_B + MIX_WIDTH
DILATED_CONFIGS = ((128, 1), (512, 4), (2048, 16))
N_CFG = len(DILATED_CONFIGS)
BLK = 128
N_BUCKETS = 32
MAX_DISTANCE = 2048
LN_EPS = 1e-5
DEEPNORM_ALPHA = (2.0 * DEPTH) ** 0.25
QK_SCALE = 1.0 / math.sqrt(HEAD_DIM)
LOG2E = math.log2(math.e)

LANES = 128
PAIRS_A = WIDTH_A // LANES
PAIRS_B = WIDTH_B // LANES
NEG_BIG = -1e30

ROW_TILE = 512
OUT_ROW_TILE = 1024
SB_BLOCK = 256
SB_SUBS = 8
SB_DEAD = -110.0
DIL_CHUNK = BLK * max(d for _, d in DILATED_CONFIGS)
DIL_BLOCKS = DIL_CHUNK // BLK
VMEM_LIMIT = 48 * 1024 * 1024

F32 = jnp.float32
BF16 = jnp.bfloat16


def _params(semantics):
    return pltpu.CompilerParams(dimension_semantics=semantics, vmem_limit_bytes=VMEM_LIMIT)


def _layer_norm_rows(x, g, b):
    mu = jnp.mean(x, axis=-1, keepdims=True)
    xc = x - mu
    var = jnp.mean(xc * xc, axis=-1, keepdims=True)
    return xc * lax.rsqrt(var + LN_EPS) * g + b


_PROJ_A = ((0, QK_SCALE * LOG2E), (WIDTH_A, 1.0), (2 * WIDTH_A, 1.0))
_PROJ_B = ((3 * WIDTH_A, QK_SCALE), (3 * WIDTH_A + WIDTH_B, 1.0), (3 * WIDTH_A + 2 * WIDTH_B, 1.0))
_GATE_COL = 3 * WIDTH_A + 3 * WIDTH_B
_DILS = tuple(d for _, d in DILATED_CONFIGS)


def _proj_kernel(*refs, normalize):
    if normalize:
        x_ref, g_ref, b_ref, w_ref, *refs = refs
        h = _layer_norm_rows(x_ref[...], g_ref[...], b_ref[...]).astype(BF16)
    else:
        h_ref, w_ref, *refs = refs
        h = h_ref[...]
    _project_rows(h, w_ref, refs)


def _project_rows(h, w_ref, refs):
    a_refs, (b_ref, gate_ref), fold_refs = refs[:N_CFG], refs[N_CFG:N_CFG + 2], refs[N_CFG + 2:]
    slabs = range(WIDTH_A // LANES)
    for t, (c0, scale) in enumerate(_PROJ_A):
        cols = slice(t * WIDTH_A, (t + 1) * WIDTH_A)
        acc = jnp.dot(h, w_ref[:, c0:c0 + WIDTH_A], preferred_element_type=F32)
        if scale != 1.0:
            acc = acc * scale
        a_refs[0][0, :, cols] = acc.astype(BF16)
        src_ref, src_dil = fold_refs[2 * t], 1
        for j in slabs:
            src_ref[j] = acc[:, j * LANES:(j + 1) * LANES]
        for cfg in range(1, N_CFG):
            dil, dst_ref = _DILS[cfg], fold_refs[2 * t + cfg % 2]
            step, rows_out = dil // src_dil, ROW_TILE // dil
            for r_src in range(src_dil):
                for a in range(step):
                    r = r_src + src_dil * a
                    rows = pl.ds(r_src * (ROW_TILE // src_dil) + a, rows_out, stride=step)
                    parts = [src_ref[j, rows, :] for j in slabs]
                    a_refs[cfg][r, :, cols] = jnp.concatenate(parts, axis=1).astype(BF16)
                    if cfg + 1 < N_CFG:
                        for j in slabs:
                            dst_ref[j, r * rows_out:(r + 1) * rows_out, :] = parts[j]
            src_ref, src_dil = dst_ref, dil
    for t, (c0, scale) in enumerate(_PROJ_B):
        acc = jnp.dot(h, w_ref[:, c0:c0 + WIDTH_B], preferred_element_type=F32)
        if scale != 1.0:
            acc = acc * scale
        b_ref[:, t * WIDTH_B:(t + 1) * WIDTH_B] = acc.astype(BF16)
    gate = jnp.dot(h, w_ref[:, _GATE_COL:_GATE_COL + MIX_WIDTH], preferred_element_type=F32)
    gate_ref[...] = (0.5 * gate).astype(BF16)


def _projection_plumbing(n, batch, layer):
    t = n // batch
    tiles = t // ROW_TILE
    assert t % ROW_TILE == 0 and all(ROW_TILE % (16 * d) == 0 for d in _DILS)
    weights = pl.BlockSpec((None, D_MODEL, IN_COLS), lambda i: (layer, 0, 0),
                           pipeline_mode=pl.Buffered(1))
    a_specs = [pl.BlockSpec((None, d, ROW_TILE // d, 3 * WIDTH_A), lambda i: (i // tiles, 0, i % tiles, 0))
               for d in _DILS]
    a_shapes = [jax.ShapeDtypeStruct((batch, d, t // d, 3 * WIDTH_A), BF16) for d in _DILS]
    b_widths = (3 * WIDTH_B, MIX_WIDTH)
    out_specs = a_specs + [pl.BlockSpec((ROW_TILE, w), lambda i: (i, 0)) for w in b_widths]
    out_shapes = a_shapes + [jax.ShapeDtypeStruct((n, w), BF16) for w in b_widths]
    scratch = [pltpu.VMEM((WIDTH_A // LANES, ROW_TILE, LANES), F32)] * (2 * len(_PROJ_A))
    return weights, out_specs, out_shapes, scratch


def _project(rows, w_bf16, layer, batch, ln0=None):
    n = rows.shape[0]
    row = pl.BlockSpec((ROW_TILE, D_MODEL), lambda i: (i, 0))
    vec = pl.BlockSpec((1, D_MODEL), lambda i: (0, 0))
    ln_args = () if ln0 is None else tuple(a.reshape(1, D_MODEL) for a in ln0)
    weights, out_specs, out_shapes, scratch = _projection_plumbing(n, batch, layer)
    outs = pl.pallas_call(
        functools.partial(_proj_kernel, normalize=ln0 is not None),
        grid=(n // ROW_TILE,),
        in_specs=[row] + [vec] * len(ln_args) + [weights],
        out_specs=out_specs,
        out_shape=out_shapes,
        scratch_shapes=scratch,
        compiler_params=_params(("parallel",)),
        name="in_proj",
    )(rows, *ln_args, w_bf16)
    return outs[:N_CFG], outs[N_CFG], outs[N_CFG + 1]


def _head_rows(rows_per_head):
    head_of_lane = np.arange(LANES) // HEAD_DIM
    head_of_row = np.arange(2 * rows_per_head) // rows_per_head
    return jnp.asarray(head_of_row[:, None] == head_of_lane[None, :], dtype=BF16)


def _merge_pair(x0, x1):
    lane = lax.broadcasted_iota(jnp.int32, x0.shape, 1)
    return jnp.where(lane < HEAD_DIM, x0, x1)


_NT = (((1,), (1,)), ((), ()))


def _t5_bucket(dist):
    max_exact = N_BUCKETS // 2
    n = np.maximum(dist, 1).astype(np.float32)
    large = max_exact + (np.log(n / max_exact) / math.log(MAX_DISTANCE / max_exact)
                         * (N_BUCKETS - max_exact)).astype(np.int32)
    large = np.minimum(large, N_BUCKETS - 1)
    return np.where(dist < max_exact, dist, large).astype(np.int32)


def _bucket_segments(window, dil):
    buckets = _t5_bucket(np.arange(window // dil + 1) * dil)
    return [(d, int(bk)) for d, bk in enumerate(buckets) if d == 0 or bk != buckets[d - 1]]


def _bias_kernel(rel_ref, out_ref):
    i = lax.broadcasted_iota(jnp.int32, (BLK, 2 * BLK), 0)
    j = lax.broadcasted_iota(jnp.int32, (BLK, 2 * BLK), 1)
    for cfg, (window, dil) in enumerate(DILATED_CONFIGS):
        segments = _bucket_segments(window, dil)
        for variant in range(2):
            delta = (BLK + i - j) if variant == 0 else (i - j)
            band = (delta >= 0) & (delta <= window // dil)
            steps = [delta >= start for start, _ in segments[1:]]
            for h in range(N_HEADS_A):
                val = jnp.full((BLK, 2 * BLK), rel_ref[segments[0][1], h] * LOG2E, F32)
                for step, (_, bucket) in zip(steps, segments[1:]):
                    val = jnp.where(step, rel_ref[bucket, h] * LOG2E, val)
                out_ref[cfg, variant, h] = jnp.where(band, val, NEG_BIG)


def _bias_tables(rel_bias):
    shape = (N_CFG, 2, N_HEADS_A, BLK, 2 * BLK)
    return pl.pallas_call(
        _bias_kernel,
        in_specs=[pl.BlockSpec(memory_space=pltpu.SMEM)],
        out_shape=jax.ShapeDtypeStruct(shape, F32),
        compiler_params=pltpu.CompilerParams(vmem_limit_bytes=VMEM_LIMIT),
        name="bias_tables",
    )(rel_bias.astype(F32))


def _dilated_kernel(*refs):
    q_refs, k_refs, v_refs = refs[:N_CFG], refs[N_CFG:2 * N_CFG], refs[2 * N_CFG:3 * N_CFG]
    bias_ref, ones_ref, o_ref, *state_refs = refs[3 * N_CFG:]
    c = pl.program_id(2)
    n_state = N_CFG - 1
    m_refs, den_refs, num_refs = state_refs[:n_state], state_refs[n_state:2 * n_state], state_refs[2 * n_state:]

    def attend(cfg, dil, r, n):
        per_res = DIL_BLOCKS // dil
        g = c * per_res + n
        k_rows = pl.ds(pl.multiple_of(jnp.maximum(g - 1, 0) * BLK, BLK), 2 * BLK)
        first = jnp.where(g == 0, 1, 0) if n == 0 else 0
        ones = ones_ref[...]
        q = q_refs[cfg][r, n * BLK:(n + 1) * BLK, :]
        q_pair = (q * ones[0:BLK], q * ones[2 * BLK:3 * BLK])
        kb = k_refs[cfg][r, k_rows, :]
        vb = v_refs[cfg][r, k_rows, :]
        v_sum = jnp.concatenate([jnp.concatenate([vb, vb], axis=0) * ones, ones], axis=1)
        ms, ps = [], []
        for h in range(2):
            s = lax.dot_general(q_pair[h], kb, _NT, preferred_element_type=F32) + bias_ref[cfg, first, h]
            m = jnp.max(s, axis=-1, keepdims=True)
            ps.append(jnp.exp2(s - m).astype(BF16))
            ms.append(jnp.broadcast_to(m, (BLK, LANES)))
        both = jnp.dot(jnp.concatenate(ps, axis=1), v_sum, preferred_element_type=F32)
        m, num, den = _merge_pair(*ms), both[:, :LANES], both[:, LANES:]
        if cfg > 0:
            rows = pl.ds(n * BLK * dil + r, BLK, stride=dil)
            m_refs[cfg - 1][rows, :], num_refs[cfg - 1][rows, :], den_refs[cfg - 1][rows, :] = m, num, den
            return
        rows = pl.ds(n * BLK, BLK)
        others = [ref[rows, :] for ref in m_refs]
        top = functools.reduce(jnp.maximum, others, m)
        w = jnp.exp2(m - top)
        num, den = w * num, w * den
        for m_c, num_ref, den_ref in zip(others, num_refs, den_refs):
            w = jnp.exp2(m_c - top)
            num, den = num + w * num_ref[rows, :], den + w * den_ref[rows, :]
        o_ref[rows, :] = (num / den).astype(o_ref.dtype)

    assert _DILS[0] == 1
    for cfg in reversed(range(N_CFG)):
        dil = _DILS[cfg]
        for r in range(dil):
            for n in range(DIL_BLOCKS // dil):
                attend(cfg, dil, r, n)


def _dilated(qkv, bias_tabs):
    b, _, t, _ = qkv[0].shape
    assert t % DIL_CHUNK == 0
    q_specs = [pl.BlockSpec((None, d, DIL_CHUNK // d, LANES), lambda bi, p, c: (bi, 0, c, p)) for d in _DILS]
    k_specs = [pl.BlockSpec((None, d, t // d, LANES), lambda bi, p, c: (bi, 0, 0, PAIRS_A + p)) for d in _DILS]
    v_specs = [pl.BlockSpec((None, d, t // d, LANES), lambda bi, p, c: (bi, 0, 0, 2 * PAIRS_A + p)) for d in _DILS]
    bias_spec = pl.BlockSpec((N_CFG, 2, 2, BLK, 2 * BLK), lambda bi, p, c: (0, 0, p, 0, 0))
    ones = _head_rows(2 * BLK)
    return pl.pallas_call(
        _dilated_kernel,
        grid=(b, PAIRS_A, t // DIL_CHUNK),
        in_specs=q_specs + k_specs + v_specs
                 + [bias_spec, pl.BlockSpec((4 * BLK, LANES), lambda bi, p, c: (0, 0))],
        out_specs=pl.BlockSpec((None, DIL_CHUNK, LANES), lambda bi, p, c: (bi, c, p)),
        out_shape=jax.ShapeDtypeStruct((b, t, WIDTH_A), BF16),
        scratch_shapes=[pltpu.VMEM((DIL_CHUNK, LANES), F32)] * (3 * (N_CFG - 1)),
        compiler_params=_params(("parallel", "parallel", "arbitrary")),
        name="dilated",
    )(*qkv, *qkv, *qkv, bias_tabs, ones)


def _softplus(z):
    return jnp.maximum(z, 0.0) + jnp.log(1.0 + jnp.exp2(jnp.abs(z) * (-LOG2E)))


def _sb_kernel(q_ref, k_ref, v_ref, tri_ref, heads_ref, o_ref, qs_ref, acc_ref, carry_ref, live_ref):
    step_id = pl.program_id(2)
    bq = SB_BLOCK
    for s in range(SB_SUBS):
        q = q_ref[s * bq:(s + 1) * bq, :]
        qs_ref[s] = jnp.concatenate([q, q], axis=0) * heads_ref[...]
    tri = tri_ref[...]

    def step(qs, j, carry):
        diagonal = carry is None
        ks = pl.multiple_of(j * bq, bq)
        kb = k_ref[pl.ds(ks, bq), :]
        vb = v_ref[pl.ds(ks, bq), :]
        z = lax.dot_general(qs, kb, _NT, preferred_element_type=F32)
        t = _softplus(z)
        if diagonal:
            row = lax.broadcasted_iota(jnp.int32, z.shape, 0) & (bq - 1)
            colv = lax.broadcasted_iota(jnp.int32, z.shape, 1)
            causal = colv < row
            t = jnp.where(causal, t, 0.0)
        after = jnp.dot(t.astype(BF16), tri, preferred_element_type=F32)
        expo = (z - t) + after
        if not diagonal:
            expo = expo + jnp.concatenate([carry] * (bq // LANES), axis=1)
        a = jnp.exp2(expo * LOG2E)
        if diagonal:
            a = jnp.where(causal, a, 0.0)
        pv = jnp.dot(a.astype(BF16), vb, preferred_element_type=F32)
        block_sum = jnp.broadcast_to(after[:, 0:1] - t[:, 0:1], (2 * bq, LANES))
        return pv, block_sum

    def run(s, first_block, count, fresh):
        qs = qs_ref[s]
        acc, carry = (None, None) if fresh else (acc_ref[s], carry_ref[s])
        for u in range(count):
            pv, block_sum = step(qs, first_block - u, carry)
            acc = pv if acc is None else acc + pv
            carry = block_sum if carry is None else carry + block_sum
        acc_ref[s], carry_ref[s] = acc, carry
        return jnp.max(carry)

    @pl.when(step_id == 0)
    def _():
        run(0, 0, 1, True)
        live_ref[0] = SB_DEAD
        for s in range(1, SB_SUBS):
            live_ref[s] = run(s, s, 2, True)

    @pl.when(step_id > 0)
    def _():
        for s in range(SB_SUBS):
            live_ref[s] = run(s, step_id * SB_SUBS + s, 2, True)

    def older(s, c):
        def more(state):
            j, live = state
            return (j >= 1) & (live > SB_DEAD)

        def pair(state):
            j, _ = state
            return j - 2, run(s, j, 2, False)

        j, live = lax.while_loop(more, pair, (step_id * SB_SUBS + s - 2, live_ref[s]))

        @pl.when((j == 0) & (live > SB_DEAD))
        def _():
            run(s, 0, 1, False)

        return c

    lax.fori_loop(0, SB_SUBS, older, 0)
    for s in range(SB_SUBS):
        o_ref[s * bq:(s + 1) * bq, :] = _merge_pair(acc_ref[s, 0:bq, :], acc_ref[s, bq:2 * bq, :]).astype(o_ref.dtype)


def _stick_breaking(qkv):
    b, t, _ = qkv.shape
    bq = SB_BLOCK
    rows = SB_SUBS * bq
    assert t % rows == 0
    r = np.arange(bq)
    tri = jnp.asarray(np.where(r[:, None] > r[None, :], -1.0, 0.0), dtype=BF16)
    heads = _head_rows(bq)
    q_spec = pl.BlockSpec((None, rows, LANES), lambda bi, p, i: (bi, i, p))
    k_spec = pl.BlockSpec((None, t, LANES), lambda bi, p, i: (bi, 0, PAIRS_B + p))
    v_spec = pl.BlockSpec((None, t, LANES), lambda bi, p, i: (bi, 0, 2 * PAIRS_B + p))
    const = lambda shape: pl.BlockSpec(shape, lambda bi, p, i: (0, 0))
    return pl.pallas_call(
        _sb_kernel,
        grid=(b, PAIRS_B, t // rows),
        in_specs=[q_spec, k_spec, v_spec, const((bq, bq)), const((2 * bq, LANES))],
        out_specs=q_spec,
        out_shape=jax.ShapeDtypeStruct((b, t, WIDTH_B), BF16),
        scratch_shapes=[pltpu.VMEM((SB_SUBS, 2 * bq, LANES), BF16), pltpu.VMEM((SB_SUBS, 2 * bq, LANES), F32),
                        pltpu.VMEM((SB_SUBS, 2 * bq, LANES), F32), pltpu.SMEM((SB_SUBS,), F32)],
        compiler_params=_params(("parallel", "parallel", "arbitrary")),
        name="stick_breaking",
    )(qkv, qkv, qkv, tri, heads)


def _out_kernel(*refs, normalize, project):
    ya_ref, yb_ref, gate_ref, h_ref, *refs = refs
    h = h_ref[...]
    if normalize:
        g0_ref, b0_ref, *refs = refs
        h = _layer_norm_rows(h, g0_ref[...], b0_ref[...])
    w_ref, b_ref, g_ref, beta_ref, *refs = refs
    if project:
        w_next_ref, *refs = refs
    hn_ref, *refs = refs
    half_gate = gate_ref[...].astype(F32)
    silu = half_gate + half_gate * jnp.tanh(half_gate)
    ya = (ya_ref[...].astype(F32) * silu[:, :WIDTH_A]).astype(BF16)
    yb = (yb_ref[...].astype(F32) * silu[:, WIDTH_A:]).astype(BF16)
    out = (jnp.dot(ya, w_ref[0:WIDTH_A, :], preferred_element_type=F32)
           + jnp.dot(yb, w_ref[WIDTH_A:MIX_WIDTH, :], preferred_element_type=F32)
           + b_ref[...])
    y = _layer_norm_rows(DEEPNORM_ALPHA * h + out, g_ref[...], beta_ref[...])
    hn_ref[...] = y
    if project:
        _project_rows(y.astype(BF16), w_next_ref, refs)


def _out_block(ya, yb, gate, h, w_out_bf16, layer, b_out, g, beta, batch, ln0=None, w_in_bf16=None):
    n = h.shape[0]
    project = w_in_bf16 is not None
    tile = ROW_TILE if project else OUT_ROW_TILE
    half = pl.BlockSpec((tile, WIDTH_A), lambda i: (i, 0))
    full = pl.BlockSpec((tile, D_MODEL), lambda i: (i, 0))
    vec = pl.BlockSpec((1, D_MODEL), lambda i: (0, 0))
    as_vec = lambda a: a.reshape(1, D_MODEL)
    ln_args = () if ln0 is None else tuple(as_vec(a) for a in ln0)
    in_specs = [half, half, full, full] + [vec] * len(ln_args) + [
        pl.BlockSpec((None, MIX_WIDTH, D_MODEL), lambda i: (layer, 0, 0), pipeline_mode=pl.Buffered(1)),
        vec, vec, vec]
    args = [ya.reshape(n, WIDTH_A), yb.reshape(n, WIDTH_B), gate, h, *ln_args, w_out_bf16,
            as_vec(b_out), as_vec(g), as_vec(beta)]
    out_specs, out_shapes, scratch = [full], [jax.ShapeDtypeStruct((n, D_MODEL), F32)], []
    if project:
        weights, proj_specs, proj_shapes, scratch = _projection_plumbing(n, batch, layer + 1)
        in_specs, args = in_specs + [weights], args + [w_in_bf16]
        out_specs, out_shapes = out_specs + proj_specs, out_shapes + proj_shapes
    outs = pl.pallas_call(
        functools.partial(_out_kernel, normalize=ln0 is not None, project=project),
        grid=(n // tile,),
        in_specs=in_specs,
        out_specs=out_specs,
        out_shape=out_shapes,
        scratch_shapes=scratch,
        compiler_params=_params(("parallel",)),
        name="out_proj_norm",
    )(*args)
    if not project:
        return outs[0], None
    return outs[0], (outs[1:1 + N_CFG], outs[1 + N_CFG], outs[2 + N_CFG])


def kernel(x, ln0_g, ln0_b, w_in, w_out, b_out, ln_g, ln_b, rel_bias):
    b, t, _ = x.shape
    n = b * t
    bias_tabs = _bias_tables(rel_bias)
    w_in, w_out = w_in.astype(BF16), w_out.astype(BF16)
    h = x.reshape(n, D_MODEL)
    ln0 = (ln0_g, ln0_b)
    projections = _project(h, w_in, 0, b, ln0)
    for l in range(DEPTH):
        qkv_a, qkv_b, gate = projections
        ya = _dilated(qkv_a, bias_tabs)
        yb = _stick_breaking(qkv_b.reshape(b, t, 3 * WIDTH_B))
        h, projections = _out_block(ya, yb, gate, h, w_out, l, b_out[l], ln_g[l], ln_b[l], b, ln0=ln0,
                                    w_in_bf16=w_in if l + 1 < DEPTH else None)
        ln0 = None
    return h.reshape(b, t, D_MODEL)
```

```python
import functools
import math

import numpy as np
import jax
import jax.numpy as jnp
from jax import lax
from jax.experimental import pallas as pl
from jax.experimental.pallas import tpu as pltpu

D_MODEL = 1024
DEPTH = 2
HEAD_DIM = 64
N_HEADS_A = 8
N_HEADS_B = 8
WIDTH_A = N_HEADS_A * HEAD_DIM
WIDTH_B = N_HEADS_B * HEAD_DIM
MIX_WIDTH = WIDTH_A + WIDTH_B
IN_COLS = 3 * WIDTH_A + 3 * WIDTH_B + MIX_WIDTH
DILATED_CONFIGS = ((128, 1), (512, 4), (2048, 16))
N_CFG = len(DILATED_CONFIGS)
BLK = 128
N_BUCKETS = 32
MAX_DISTANCE = 2048
LN_EPS = 1e-5
DEEPNORM_ALPHA = (2.0 * DEPTH) ** 0.25
QK_SCALE = 1.0 / math.sqrt(HEAD_DIM)
LOG2E = math.log2(math.e)

LANES = 128
PAIRS_A = WIDTH_A // LANES
PAIRS_B = WIDTH_B // LANES
NEG_BIG = -1e30

ROW_TILE = 512
OUT_ROW_TILE = 1024
SB_BLOCK = 256
SB_SUBS = 8
SB_DEAD = -110.0
DIL_CHUNK = BLK * max(d for _, d in DILATED_CONFIGS)
DIL_BLOCKS = DIL_CHUNK // BLK
VMEM_LIMIT = 48 * 1024 * 1024

F32 = jnp.float32
BF16 = jnp.bfloat16


def _params(semantics):
    return pltpu.CompilerParams(dimension_semantics=semantics, vmem_limit_bytes=VMEM_LIMIT)


def _layer_norm_rows(x, g, b):
    mu = jnp.mean(x, axis=-1, keepdims=True)
    xc = x - mu
    var = jnp.mean(xc * xc, axis=-1, keepdims=True)
    return xc * lax.rsqrt(var + LN_EPS) * g + b


_PROJ_A = ((0, QK_SCALE * LOG2E), (WIDTH_A, 1.0), (2 * WIDTH_A, 1.0))
_PROJ_B = ((3 * WIDTH_A, QK_SCALE), (3 * WIDTH_A + WIDTH_B, 1.0), (3 * WIDTH_A + 2 * WIDTH_B, 1.0))
_GATE_COL = 3 * WIDTH_A + 3 * WIDTH_B
_DILS = tuple(d for _, d in DILATED_CONFIGS)


def _proj_kernel(*refs, normalize):
    if normalize:
        x_ref, g_ref, b_ref, w_ref, *refs = refs
        h = _layer_norm_rows(x_ref[...], g_ref[...], b_ref[...]).astype(BF16)
    else:
        h_ref, w_ref, *refs = refs
        h = h_ref[...]
    _project_rows(h, w_ref, refs)


def _project_rows(h, w_ref, refs):
    a_refs, (b_ref, gate_ref), fold_refs = refs[:N_CFG], refs[N_CFG:N_CFG + 2], refs[N_CFG + 2:]
    slabs = range(WIDTH_A // LANES)
    for t, (c0, scale) in enumerate(_PROJ_A):
        cols = slice(t * WIDTH_A, (t + 1) * WIDTH_A)
        acc = jnp.dot(h, w_ref[:, c0:c0 + WIDTH_A], preferred_element_type=F32)
        if scale != 1.0:
            acc = acc * scale
        a_refs[0][0, :, cols] = acc.astype(BF16)
        src_ref, src_dil = fold_refs[2 * t], 1
        for j in slabs:
            src_ref[j] = acc[:, j * LANES:(j + 1) * LANES]
        for cfg in range(1, N_CFG):
            dil, dst_ref = _DILS[cfg], fold_refs[2 * t + cfg % 2]
            step, rows_out = dil // src_dil, ROW_TILE // dil
            for r_src in range(src_dil):
                for a in range(step):
                    r = r_src + src_dil * a
                    rows = pl.ds(r_src * (ROW_TILE // src_dil) + a, rows_out, stride=step)
                    parts = [src_ref[j, rows, :] for j in slabs]
                    a_refs[cfg][r, :, cols] = jnp.concatenate(parts, axis=1).astype(BF16)
                    if cfg + 1 < N_CFG:
                        for j in slabs:
                            dst_ref[j, r * rows_out:(r + 1) * rows_out, :] = parts[j]
            src_ref, src_dil = dst_ref, dil
    for t, (c0, scale) in enumerate(_PROJ_B):
        acc = jnp.dot(h, w_ref[:, c0:c0 + WIDTH_B], preferred_element_type=F32)
        if scale != 1.0:
            acc = acc * scale
        b_ref[:, t * WIDTH_B:(t + 1) * WIDTH_B] = acc.astype(BF16)
    gate = jnp.dot(h, w_ref[:, _GATE_COL:_GATE_COL + MIX_WIDTH], preferred_element_type=F32)
    gate_ref[...] = (0.5 * gate).astype(BF16)


def _projection_plumbing(n, batch, layer):
    t = n // batch
    tiles = t // ROW_TILE
    assert t % ROW_TILE == 0 and all(ROW_TILE % (16 * d) == 0 for d in _DILS)
    weights = pl.BlockSpec((None, D_MODEL, IN_COLS), lambda i: (layer, 0, 0),
                           pipeline_mode=pl.Buffered(1))
    a_specs = [pl.BlockSpec((None, d, ROW_TILE // d, 3 * WIDTH_A), lambda i: (i // tiles, 0, i % tiles, 0))
               for d in _DILS]
    a_shapes = [jax.ShapeDtypeStruct((batch, d, t // d, 3 * WIDTH_A), BF16) for d in _DILS]
    b_widths = (3 * WIDTH_B, MIX_WIDTH)
    out_specs = a_specs + [pl.BlockSpec((ROW_TILE, w), lambda i: (i, 0)) for w in b_widths]
    out_shapes = a_shapes + [jax.ShapeDtypeStruct((n, w), BF16) for w in b_widths]
    scratch = [pltpu.VMEM((WIDTH_A // LANES, ROW_TILE, LANES), F32)] * (2 * len(_PROJ_A))
    return weights, out_specs, out_shapes, scratch


def _project(rows, w_bf16, layer, batch, ln0=None):
    n = rows.shape[0]
    row = pl.BlockSpec((ROW_TILE, D_MODEL), lambda i: (i, 0))
    vec = pl.BlockSpec((1, D_MODEL), lambda i: (0, 0))
    ln_args = () if ln0 is None else tuple(a.reshape(1, D_MODEL) for a in ln0)
    weights, out_specs, out_shapes, scratch = _projection_plumbing(n, batch, layer)
    outs = pl.pallas_call(
        functools.partial(_proj_kernel, normalize=ln0 is not None),
        grid=(n // ROW_TILE,),
        in_specs=[row] + [vec] * len(ln_args) + [weights],
        out_specs=out_specs,
        out_shape=out_shapes,
        scratch_shapes=scratch,
        compiler_params=_params(("parallel",)),
        name="in_proj",
    )(rows, *ln_args, w_bf16)
    return outs[:N_CFG], outs[N_CFG], outs[N_CFG + 1]


def _head_rows(rows_per_head):
    head_of_lane = np.arange(LANES) // HEAD_DIM
    head_of_row = np.arange(2 * rows_per_head) // rows_per_head
    return jnp.asarray(head_of_row[:, None] == head_of_lane[None, :], dtype=BF16)


def _merge_pair(x0, x1):
    lane = lax.broadcasted_iota(jnp.int32, x0.shape, 1)
    return jnp.where(lane < HEAD_DIM, x0, x1)


_NT = (((1,), (1,)), ((), ()))


def _t5_bucket(dist):
    max_exact = N_BUCKETS // 2
    n = np.maximum(dist, 1).astype(np.float32)
    large = max_exact + (np.log(n / max_exact) / math.log(MAX_DISTANCE / max_exact)
                         * (N_BUCKETS - max_exact)).astype(np.int32)
    large = np.minimum(large, N_BUCKETS - 1)
    return np.where(dist < max_exact, dist, large).astype(np.int32)


def _bucket_segments(window, dil):
    buckets = _t5_bucket(np.arange(window // dil + 1) * dil)
    return [(d, int(bk)) for d, bk in enumerate(buckets) if d == 0 or bk != buckets[d - 1]]


def _bias_kernel(rel_ref, out_ref):
    i = lax.broadcasted_iota(jnp.int32, (BLK, 2 * BLK), 0)
    j = lax.broadcasted_iota(jnp.int32, (BLK, 2 * BLK), 1)
    for cfg, (window, dil) in enumerate(DILATED_CONFIGS):
        segments = _bucket_segments(window, dil)
        for variant in range(2):
            delta = (BLK + i - j) if variant == 0 else (i - j)
            band = (delta >= 0) & (delta <= window // dil)
            steps = [delta >= start for start, _ in segments[1:]]
            for h in range(N_HEADS_A):
                val = jnp.full((BLK, 2 * BLK), rel_ref[segments[0][1], h] * LOG2E, F32)
                for step, (_, bucket) in zip(steps, segments[1:]):
                    val = jnp.where(step, rel_ref[bucket, h] * LOG2E, val)
                out_ref[cfg, variant, h] = jnp.where(band, val, NEG_BIG)


def _bias_tables(rel_bias):
    shape = (N_CFG, 2, N_HEADS_A, BLK, 2 * BLK)
    return pl.pallas_call(
        _bias_kernel,
        in_specs=[pl.BlockSpec(memory_space=pltpu.SMEM)],
        out_shape=jax.ShapeDtypeStruct(shape, F32),
        compiler_params=pltpu.CompilerParams(vmem_limit_bytes=VMEM_LIMIT),
        name="bias_tables",
    )(rel_bias.astype(F32))


def _dilated_kernel(*refs):
    q_refs, k_refs, v_refs = refs[:N_CFG], refs[N_CFG:2 * N_CFG], refs[2 * N_CFG:3 * N_CFG]
    bias_ref, ones_ref, o_ref, *state_refs = refs[3 * N_CFG:]
    c = pl.program_id(2)
    n_state = N_CFG - 1
    m_refs, den_refs, num_refs = state_refs[:n_state], state_refs[n_state:2 * n_state], state_refs[2 * n_state:]

    def attend(cfg, dil, r, n):
        per_res = DIL_BLOCKS // dil
        g = c * per_res + n
        k_rows = pl.ds(pl.multiple_of(jnp.maximum(g - 1, 0) * BLK, BLK), 2 * BLK)
        first = jnp.where(g == 0, 1, 0) if n == 0 else 0
        ones = ones_ref[...]
        q = q_refs[cfg][r, n * BLK:(n + 1) * BLK, :]
        q_pair = (q * ones[0:BLK], q * ones[2 * BLK:3 * BLK])
        kb = k_refs[cfg][r, k_rows, :]
        vb = v_refs[cfg][r, k_rows, :]
        v_sum = jnp.concatenate([jnp.concatenate([vb, vb], axis=0) * ones, ones], axis=1)
        ms, ps = [], []
        for h in range(2):
            s = lax.dot_general(q_pair[h], kb, _NT, preferred_element_type=F32) + bias_ref[cfg, first, h]
            m = jnp.max(s, axis=-1, keepdims=True)
            ps.append(jnp.exp2(s - m).astype(BF16))
            ms.append(jnp.broadcast_to(m, (BLK, LANES)))
        both = jnp.dot(jnp.concatenate(ps, axis=1), v_sum, preferred_element_type=F32)
        m, num, den = _merge_pair(*ms), both[:, :LANES], both[:, LANES:]
        if cfg > 0:
            rows = pl.ds(n * BLK * dil + r, BLK, stride=dil)
            m_refs[cfg - 1][rows, :], num_refs[cfg - 1][rows, :], den_refs[cfg - 1][rows, :] = m, num, den
            return
        rows = pl.ds(n * BLK, BLK)
        others = [ref[rows, :] for ref in m_refs]
        top = functools.reduce(jnp.maximum, others, m)
        w = jnp.exp2(m - top)
        num, den = w * num, w * den
        for m_c, num_ref, den_ref in zip(others, num_refs, den_refs):
            w = jnp.exp2(m_c - top)
            num, den = num + w * num_ref[rows, :], den + w * den_ref[rows, :]
        o_ref[rows, :] = (num / den).astype(o_ref.dtype)

    assert _DILS[0] == 1
    for cfg in reversed(range(N_CFG)):
        dil = _DILS[cfg]
        for r in range(dil):
            for n in range(DIL_BLOCKS // dil):
                attend(cfg, dil, r, n)


def _dilated(qkv, bias_tabs):
    b, _, t, _ = qkv[0].shape
    assert t % DIL_CHUNK == 0
    q_specs = [pl.BlockSpec((None, d, DIL_CHUNK // d, LANES), lambda bi, p, c: (bi, 0, c, p)) for d in _DILS]
    k_specs = [pl.BlockSpec((None, d, t // d, LANES), lambda bi, p, c: (bi, 0, 0, PAIRS_A + p)) for d in _DILS]
    v_specs = [pl.BlockSpec((None, d, t // d, LANES), lambda bi, p, c: (bi, 0, 0, 2 * PAIRS_A + p)) for d in _DILS]
    bias_spec = pl.BlockSpec((N_CFG, 2, 2, BLK, 2 * BLK), lambda bi, p, c: (0, 0, p, 0, 0))
    ones = _head_rows(2 * BLK)
    return pl.pallas_call(
        _dilated_kernel,
        grid=(b, PAIRS_A, t // DIL_CHUNK),
        in_specs=q_specs + k_specs + v_specs
                 + [bias_spec, pl.BlockSpec((4 * BLK, LANES), lambda bi, p, c: (0, 0))],
        out_specs=pl.BlockSpec((None, DIL_CHUNK, LANES), lambda bi, p, c: (bi, c, p)),
        out_shape=jax.ShapeDtypeStruct((b, t, WIDTH_A), BF16),
        scratch_shapes=[pltpu.VMEM((DIL_CHUNK, LANES), F32)] * (3 * (N_CFG - 1)),
        compiler_params=_params(("parallel", "parallel", "arbitrary")),
        name="dilated",
    )(*qkv, *qkv, *qkv, bias_tabs, ones)


def _softplus(z):
    return jnp.maximum(z, 0.0) + jnp.log(1.0 + jnp.exp2(jnp.abs(z) * (-LOG2E)))


def _sb_kernel(q_ref, k_ref, v_ref, tri_ref, heads_ref, o_ref, qs_ref, acc_ref, carry_ref, live_ref):
    step_id = pl.program_id(2)
    bq = SB_BLOCK
    for s in range(SB_SUBS):
        q = q_ref[s * bq:(s + 1) * bq, :]
        qs_ref[s] = jnp.concatenate([q, q], axis=0) * heads_ref[...]
    tri = tri_ref[...]

    def step(qs, j, carry):
        diagonal = carry is None
        ks = pl.multiple_of(j * bq, bq)
        kb = k_ref[pl.ds(ks, bq), :]
        vb = v_ref[pl.ds(ks, bq), :]
        z = lax.dot_general(qs, kb, _NT, preferred_element_type=F32)
        t = _softplus(z)
        if diagonal:
            row = lax.broadcasted_iota(jnp.int32, z.shape, 0) & (bq - 1)
            colv = lax.broadcasted_iota(jnp.int32, z.shape, 1)
            causal = colv < row
            t = jnp.where(causal, t, 0.0)
        after = jnp.dot(t.astype(BF16), tri, preferred_element_type=F32)
        expo = (z - t) + after
        if not diagonal:
            expo = expo + jnp.concatenate([carry] * (bq // LANES), axis=1)
        a = jnp.exp2(expo * LOG2E)
        if diagonal:
            a = jnp.where(causal, a, 0.0)
        pv = jnp.dot(a.astype(BF16), vb, preferred_element_type=F32)
        block_sum = jnp.broadcast_to(after[:, 0:1] - t[:, 0:1], (2 * bq, LANES))
        return pv, block_sum

    def run(s, first_block, count, fresh):
        qs = qs_ref[s]
        acc, carry = (None, None) if fresh else (acc_ref[s], carry_ref[s])
        for u in range(count):
            pv, block_sum = step(qs, first_block - u, carry)
            acc = pv if acc is None else acc + pv
            carry = block_sum if carry is None else carry + block_sum
        acc_ref[s], carry_ref[s] = acc, carry
        return jnp.max(carry)

    @pl.when(step_id == 0)
    def _():
        run(0, 0, 1, True)
        live_ref[0] = SB_DEAD
        for s in range(1, SB_SUBS):
            live_ref[s] = run(s, s, 2, True)

    @pl.when(step_id > 0)
    def _():
        for s in range(SB_SUBS):
            live_ref[s] = run(s, step_id * SB_SUBS + s, 2, True)

    def older(s, c):
        def more(state):
            j, live = state
            return (j >= 1) & (live > SB_DEAD)

        def pair(state):
            j, _ = state
            return j - 2, run(s, j, 2, False)

        j, live = lax.while_loop(more, pair, (step_id * SB_SUBS + s - 2, live_ref[s]))

        @pl.when((j == 0) & (live > SB_DEAD))
        def _():
            run(s, 0, 1, False)

        return c

    @pl.when(functools.reduce(jnp.maximum, [live_ref[s] for s in range(SB_SUBS)]) > SB_DEAD)
    def _():
        lax.fori_loop(0, SB_SUBS, older, 0)

    for s in range(SB_SUBS):
        o_ref[s * bq:(s + 1) * bq, :] = _merge_pair(acc_ref[s, 0:bq, :], acc_ref[s, bq:2 * bq, :]).astype(o_ref.dtype)


def _stick_breaking(qkv):
    b, t, _ = qkv.shape
    bq = SB_BLOCK
    rows = SB_SUBS * bq
    assert t % rows == 0
    r = np.arange(bq)
    tri = jnp.asarray(np.where(r[:, None] > r[None, :], -1.0, 0.0), dtype=BF16)
    heads = _head_rows(bq)
    q_spec = pl.BlockSpec((None, rows, LANES), lambda bi, p, i: (bi, i, p))
    k_spec = pl.BlockSpec((None, t, LANES), lambda bi, p, i: (bi, 0, PAIRS_B + p))
    v_spec = pl.BlockSpec((None, t, LANES), lambda bi, p, i: (bi, 0, 2 * PAIRS_B + p))
    const = lambda shape: pl.BlockSpec(shape, lambda bi, p, i: (0, 0))
    return pl.pallas_call(
        _sb_kernel,
        grid=(b, PAIRS_B, t // rows),
        in_specs=[q_spec, k_spec, v_spec, const((bq, bq)), const((2 * bq, LANES))],
        out_specs=q_spec,
        out_shape=jax.ShapeDtypeStruct((b, t, WIDTH_B), BF16),
        scratch_shapes=[pltpu.VMEM((SB_SUBS, 2 * bq, LANES), BF16), pltpu.VMEM((SB_SUBS, 2 * bq, LANES), F32),
                        pltpu.VMEM((SB_SUBS, 2 * bq, LANES), F32), pltpu.SMEM((SB_SUBS,), F32)],
        compiler_params=_params(("parallel", "parallel", "arbitrary")),
        name="stick_breaking",
    )(qkv, qkv, qkv, tri, heads)


def _out_kernel(*refs, normalize, project):
    ya_ref, yb_ref, gate_ref, h_ref, *refs = refs
    h = h_ref[...]
    if normalize:
        g0_ref, b0_ref, *refs = refs
        h = _layer_norm_rows(h, g0_ref[...], b0_ref[...])
    w_ref, b_ref, g_ref, beta_ref, *refs = refs
    if project:
        w_next_ref, *refs = refs
    hn_ref, *refs = refs
    half_gate = gate_ref[...].astype(F32)
    silu = half_gate + half_gate * jnp.tanh(half_gate)
    ya = (ya_ref[...].astype(F32) * silu[:, :WIDTH_A]).astype(BF16)
    yb = (yb_ref[...].astype(F32) * silu[:, WIDTH_A:]).astype(BF16)
    out = (jnp.dot(ya, w_ref[0:WIDTH_A, :], preferred_element_type=F32)
           + jnp.dot(yb, w_ref[WIDTH_A:MIX_WIDTH, :], preferred_element_type=F32)
           + b_ref[...])
    y = _layer_norm_rows(DEEPNORM_ALPHA * h + out, g_ref[...], beta_ref[...])
    hn_ref[...] = y
    if project:
        _project_rows(y.astype(BF16), w_next_ref, refs)


def _out_block(ya, yb, gate, h, w_out_bf16, layer, b_out, g, beta, batch, ln0=None, w_in_bf16=None):
    n = h.shape[0]
    project = w_in_bf16 is not None
    tile = ROW_TILE if project else OUT_ROW_TILE
    half = pl.BlockSpec((tile, WIDTH_A), lambda i: (i, 0))
    full = pl.BlockSpec((tile, D_MODEL), lambda i: (i, 0))
    vec = pl.BlockSpec((1, D_MODEL), lambda i: (0, 0))
    as_vec = lambda a: a.reshape(1, D_MODEL)
    ln_args = () if ln0 is None else tuple(as_vec(a) for a in ln0)
    in_specs = [half, half, full, full] + [vec] * len(ln_args) + [
        pl.BlockSpec((None, MIX_WIDTH, D_MODEL), lambda i: (layer, 0, 0), pipeline_mode=pl.Buffered(1)),
        vec, vec, vec]
    args = [ya.reshape(n, WIDTH_A), yb.reshape(n, WIDTH_B), gate, h, *ln_args, w_out_bf16,
            as_vec(b_out), as_vec(g), as_vec(beta)]
    out_specs, out_shapes, scratch = [full], [jax.ShapeDtypeStruct((n, D_MODEL), F32)], []
    if project:
        weights, proj_specs, proj_shapes, scratch = _projection_plumbing(n, batch, layer + 1)
        in_specs, args = in_specs + [weights], args + [w_in_bf16]
        out_specs, out_shapes = out_specs + proj_specs, out_shapes + proj_shapes
    outs = pl.pallas_call(
        functools.partial(_out_kernel, normalize=ln0 is not None, project=project),
        grid=(n // tile,),
        in_specs=in_specs,
        out_specs=out_specs,
        out_shape=out_shapes,
        scratch_shapes=scratch,
        compiler_params=_params(("parallel",)),
        name="out_proj_norm",
    )(*args)
    if not project:
        return outs[0], None
    return outs[0], (outs[1:1 + N_CFG], outs[1 + N_CFG], outs[2 + N_CFG])


def kernel(x, ln0_g, ln0_b, w_in, w_out, b_out, ln_g, ln_b, rel_bias):
    b, t, _ = x.shape
    n = b * t
    bias_tabs = _bias_tables(rel_bias)
    w_in, w_out = w_in.astype(BF16), w_out.astype(BF16)
    h = x.reshape(n, D_MODEL)
    ln0 = (ln0_g, ln0_b)
    projections = _project(h, w_in, 0, b, ln0)
    for l in range(DEPTH):
        qkv_a, qkv_b, gate = projections
        ya = _dilated(qkv_a, bias_tabs)
        yb = _stick_breaking(qkv_b.reshape(b, t, 3 * WIDTH_B))
        h, projections = _out_block(ya, yb, gate, h, w_out, l, b_out[l], ln_g[l], ln_b[l], b, ln0=ln0,
                                    w_in_bf16=w_in if l + 1 < DEPTH else None)
        ln0 = None
    return h.reshape(b, t, D_MODEL)
```

```python
import functools
import math

import numpy as np
import jax
import jax.numpy as jnp
from jax import lax
from jax.experimental import pallas as pl
from jax.experimental.pallas import tpu as pltpu

D_MODEL = 1024
DEPTH = 2
HEAD_DIM = 64
N_HEADS_A = 8
N_HEADS_B = 8
WIDTH_A = N_HEADS_A * HEAD_DIM
WIDTH_B = N_HEADS_B * HEAD_DIM
MIX_WIDTH = WIDTH_A + WIDTH_B
IN_COLS = 3 * WIDTH_A + 3 * WIDTH_B + MIX_WIDTH
DILATED_CONFIGS = ((128, 1), (512, 4), (2048, 16))
N_CFG = len(DILATED_CONFIGS)
BLK = 128
N_BUCKETS = 32
MAX_DISTANCE = 2048
LN_EPS = 1e-5
DEEPNORM_ALPHA = (2.0 * DEPTH) ** 0.25
QK_SCALE = 1.0 / math.sqrt(HEAD_DIM)
LOG2E = math.log2(math.e)

LANES = 128
PAIRS_A = WIDTH_A // LANES
PAIRS_B = WIDTH_B // LANES
NEG_BIG = -1e30

ROW_TILE = 512
OUT_ROW_TILE = 1024
SB_BLOCK = 256
SB_SUBS = 16
SB_DEAD = -110.0
DIL_CHUNK = BLK * max(d for _, d in DILATED_CONFIGS)
DIL_BLOCKS = DIL_CHUNK // BLK
VMEM_LIMIT = 48 * 1024 * 1024

F32 = jnp.float32
BF16 = jnp.bfloat16


def _params(semantics):
    return pltpu.CompilerParams(dimension_semantics=semantics, vmem_limit_bytes=VMEM_LIMIT)


def _layer_norm_rows(x, g, b):
    mu = jnp.mean(x, axis=-1, keepdims=True)
    xc = x - mu
    var = jnp.mean(xc * xc, axis=-1, keepdims=True)
    return xc * lax.rsqrt(var + LN_EPS) * g + b


_PROJ_A = ((0, QK_SCALE * LOG2E), (WIDTH_A, 1.0), (2 * WIDTH_A, 1.0))
_PROJ_B = ((3 * WIDTH_A, QK_SCALE), (3 * WIDTH_A + WIDTH_B, 1.0), (3 * WIDTH_A + 2 * WIDTH_B, 1.0))
_GATE_COL = 3 * WIDTH_A + 3 * WIDTH_B
_DILS = tuple(d for _, d in DILATED_CONFIGS)


def _proj_kernel(*refs, normalize):
    if normalize:
        x_ref, g_ref, b_ref, w_ref, *refs = refs
        h = _layer_norm_rows(x_ref[...], g_ref[...], b_ref[...]).astype(BF16)
    else:
        h_ref, w_ref, *refs = refs
        h = h_ref[...]
    _project_rows(h, w_ref, refs)


def _project_rows(h, w_ref, refs):
    a_refs, (b_ref, gate_ref), fold_refs = refs[:N_CFG], refs[N_CFG:N_CFG + 2], refs[N_CFG + 2:]
    slabs = range(WIDTH_A // LANES)
    for t, (c0, scale) in enumerate(_PROJ_A):
        cols = slice(t * WIDTH_A, (t + 1) * WIDTH_A)
        acc = jnp.dot(h, w_ref[:, c0:c0 + WIDTH_A], preferred_element_type=F32)
        if scale != 1.0:
            acc = acc * scale
        a_refs[0][0, :, cols] = acc.astype(BF16)
        src_ref, src_dil = fold_refs[2 * t], 1
        for j in slabs:
            src_ref[j] = acc[:, j * LANES:(j + 1) * LANES]
        for cfg in range(1, N_CFG):
            dil, dst_ref = _DILS[cfg], fold_refs[2 * t + cfg % 2]
            step, rows_out = dil // src_dil, ROW_TILE // dil
            for r_src in range(src_dil):
                for a in range(step):
                    r = r_src + src_dil * a
                    rows = pl.ds(r_src * (ROW_TILE // src_dil) + a, rows_out, stride=step)
                    parts = [src_ref[j, rows, :] for j in slabs]
                    a_refs[cfg][r, :, cols] = jnp.concatenate(parts, axis=1).astype(BF16)
                    if cfg + 1 < N_CFG:
                        for j in slabs:
                            dst_ref[j, r * rows_out:(r + 1) * rows_out, :] = parts[j]
            src_ref, src_dil = dst_ref, dil
    for t, (c0, scale) in enumerate(_PROJ_B):
        acc = jnp.dot(h, w_ref[:, c0:c0 + WIDTH_B], preferred_element_type=F32)
        if scale != 1.0:
            acc = acc * scale
        b_ref[:, t * WIDTH_B:(t + 1) * WIDTH_B] = acc.astype(BF16)
    gate = jnp.dot(h, w_ref[:, _GATE_COL:_GATE_COL + MIX_WIDTH], preferred_element_type=F32)
    gate_ref[...] = (0.5 * gate).astype(BF16)


def _projection_plumbing(n, batch, layer):
    t = n // batch
    tiles = t // ROW_TILE
    assert t % ROW_TILE == 0 and all(ROW_TILE % (16 * d) == 0 for d in _DILS)
    weights = pl.BlockSpec((None, D_MODEL, IN_COLS), lambda i: (layer, 0, 0),
                           pipeline_mode=pl.Buffered(1))
    a_specs = [pl.BlockSpec((None, d, ROW_TILE // d, 3 * WIDTH_A), lambda i: (i // tiles, 0, i % tiles, 0))
               for d in _DILS]
    a_shapes = [jax.ShapeDtypeStruct((batch, d, t // d, 3 * WIDTH_A), BF16) for d in _DILS]
    b_widths = (3 * WIDTH_B, MIX_WIDTH)
    out_specs = a_specs + [pl.BlockSpec((ROW_TILE, w), lambda i: (i, 0)) for w in b_widths]
    out_shapes = a_shapes + [jax.ShapeDtypeStruct((n, w), BF16) for w in b_widths]
    scratch = [pltpu.VMEM((WIDTH_A // LANES, ROW_TILE, LANES), F32)] * (2 * len(_PROJ_A))
    return weights, out_specs, out_shapes, scratch


def _project(rows, w_bf16, layer, batch, ln0=None):
    n = rows.shape[0]
    row = pl.BlockSpec((ROW_TILE, D_MODEL), lambda i: (i, 0))
    vec = pl.BlockSpec((1, D_MODEL), lambda i: (0, 0))
    ln_args = () if ln0 is None else tuple(a.reshape(1, D_MODEL) for a in ln0)
    weights, out_specs, out_shapes, scratch = _projection_plumbing(n, batch, layer)
    outs = pl.pallas_call(
        functools.partial(_proj_kernel, normalize=ln0 is not None),
        grid=(n // ROW_TILE,),
        in_specs=[row] + [vec] * len(ln_args) + [weights],
        out_specs=out_specs,
        out_shape=out_shapes,
        scratch_shapes=scratch,
        compiler_params=_params(("parallel",)),
        name="in_proj",
    )(rows, *ln_args, w_bf16)
    return outs[:N_CFG], outs[N_CFG], outs[N_CFG + 1]


def _head_rows(rows_per_head):
    head_of_lane = np.arange(LANES) // HEAD_DIM
    head_of_row = np.arange(2 * rows_per_head) // rows_per_head
    return jnp.asarray(head_of_row[:, None] == head_of_lane[None, :], dtype=BF16)


def _merge_pair(x0, x1):
    lane = lax.broadcasted_iota(jnp.int32, x0.shape, 1)
    return jnp.where(lane < HEAD_DIM, x0, x1)


_NT = (((1,), (1,)), ((), ()))


def _t5_bucket(dist):
    max_exact = N_BUCKETS // 2
    n = np.maximum(dist, 1).astype(np.float32)
    large = max_exact + (np.log(n / max_exact) / math.log(MAX_DISTANCE / max_exact)
                         * (N_BUCKETS - max_exact)).astype(np.int32)
    large = np.minimum(large, N_BUCKETS - 1)
    return np.where(dist < max_exact, dist, large).astype(np.int32)


def _bucket_segments(window, dil):
    buckets = _t5_bucket(np.arange(window // dil + 1) * dil)
    return [(d, int(bk)) for d, bk in enumerate(buckets) if d == 0 or bk != buckets[d - 1]]


def _bias_kernel(rel_ref, out_ref):
    i = lax.broadcasted_iota(jnp.int32, (BLK, 2 * BLK), 0)
    j = lax.broadcasted_iota(jnp.int32, (BLK, 2 * BLK), 1)
    for cfg, (window, dil) in enumerate(DILATED_CONFIGS):
        segments = _bucket_segments(window, dil)
        for variant in range(2):
            delta = (BLK + i - j) if variant == 0 else (i - j)
            band = (delta >= 0) & (delta <= window // dil)
            steps = [delta >= start for start, _ in segments[1:]]
            for h in range(N_HEADS_A):
                val = jnp.full((BLK, 2 * BLK), rel_ref[segments[0][1], h] * LOG2E, F32)
                for step, (_, bucket) in zip(steps, segments[1:]):
                    val = jnp.where(step, rel_ref[bucket, h] * LOG2E, val)
                out_ref[cfg, variant, h] = jnp.where(band, val, NEG_BIG)


def _bias_tables(rel_bias):
    shape = (N_CFG, 2, N_HEADS_A, BLK, 2 * BLK)
    return pl.pallas_call(
        _bias_kernel,
        in_specs=[pl.BlockSpec(memory_space=pltpu.SMEM)],
        out_shape=jax.ShapeDtypeStruct(shape, F32),
        compiler_params=pltpu.CompilerParams(vmem_limit_bytes=VMEM_LIMIT),
        name="bias_tables",
    )(rel_bias.astype(F32))


def _dilated_kernel(*refs):
    q_refs, k_refs, v_refs = refs[:N_CFG], refs[N_CFG:2 * N_CFG], refs[2 * N_CFG:3 * N_CFG]
    bias_ref, ones_ref, o_ref, *state_refs = refs[3 * N_CFG:]
    c = pl.program_id(2)
    n_state = N_CFG - 1
    m_refs, den_refs, num_refs = state_refs[:n_state], state_refs[n_state:2 * n_state], state_refs[2 * n_state:]

    def attend(cfg, dil, r, n):
        per_res = DIL_BLOCKS // dil
        g = c * per_res + n
        k_rows = pl.ds(pl.multiple_of(jnp.maximum(g - 1, 0) * BLK, BLK), 2 * BLK)
        first = jnp.where(g == 0, 1, 0) if n == 0 else 0
        ones = ones_ref[...]
        q = q_refs[cfg][r, n * BLK:(n + 1) * BLK, :]
        q_pair = (q * ones[0:BLK], q * ones[2 * BLK:3 * BLK])
        kb = k_refs[cfg][r, k_rows, :]
        vb = v_refs[cfg][r, k_rows, :]
        v_sum = jnp.concatenate([jnp.concatenate([vb, vb], axis=0) * ones, ones], axis=1)
        ms, ps = [], []
        for h in range(2):
            s = lax.dot_general(q_pair[h], kb, _NT, preferred_element_type=F32) + bias_ref[cfg, first, h]
            m = jnp.max(s, axis=-1, keepdims=True)
            ps.append(jnp.exp2(s - m).astype(BF16))
            ms.append(jnp.broadcast_to(m, (BLK, LANES)))
        both = jnp.dot(jnp.concatenate(ps, axis=1), v_sum, preferred_element_type=F32)
        m, num, den = _merge_pair(*ms), both[:, :LANES], both[:, LANES:]
        if cfg > 0:
            rows = pl.ds(n * BLK * dil + r, BLK, stride=dil)
            m_refs[cfg - 1][rows, :], num_refs[cfg - 1][rows, :], den_refs[cfg - 1][rows, :] = m, num, den
            return
        rows = pl.ds(n * BLK, BLK)
        others = [ref[rows, :] for ref in m_refs]
        top = functools.reduce(jnp.maximum, others, m)
        w = jnp.exp2(m - top)
        num, den = w * num, w * den
        for m_c, num_ref, den_ref in zip(others, num_refs, den_refs):
            w = jnp.exp2(m_c - top)
            num, den = num + w * num_ref[rows, :], den + w * den_ref[rows, :]
        o_ref[rows, :] = (num / den).astype(o_ref.dtype)

    assert _DILS[0] == 1
    for cfg in reversed(range(N_CFG)):
        dil = _DILS[cfg]
        for r in range(dil):
            for n in range(DIL_BLOCKS // dil):
                attend(cfg, dil, r, n)


def _dilated(qkv, bias_tabs):
    b, _, t, _ = qkv[0].shape
    assert t % DIL_CHUNK == 0
    q_specs = [pl.BlockSpec((None, d, DIL_CHUNK // d, LANES), lambda bi, p, c: (bi, 0, c, p)) for d in _DILS]
    k_specs = [pl.BlockSpec((None, d, t // d, LANES), lambda bi, p, c: (bi, 0, 0, PAIRS_A + p)) for d in _DILS]
    v_specs = [pl.BlockSpec((None, d, t // d, LANES), lambda bi, p, c: (bi, 0, 0, 2 * PAIRS_A + p)) for d in _DILS]
    bias_spec = pl.BlockSpec((N_CFG, 2, 2, BLK, 2 * BLK), lambda bi, p, c: (0, 0, p, 0, 0))
    ones = _head_rows(2 * BLK)
    return pl.pallas_call(
        _dilated_kernel,
        grid=(b, PAIRS_A, t // DIL_CHUNK),
        in_specs=q_specs + k_specs + v_specs
                 + [bias_spec, pl.BlockSpec((4 * BLK, LANES), lambda bi, p, c: (0, 0))],
        out_specs=pl.BlockSpec((None, DIL_CHUNK, LANES), lambda bi, p, c: (bi, c, p)),
        out_shape=jax.ShapeDtypeStruct((b, t, WIDTH_A), BF16),
        scratch_shapes=[pltpu.VMEM((DIL_CHUNK, LANES), F32)] * (3 * (N_CFG - 1)),
        compiler_params=_params(("parallel", "parallel", "arbitrary")),
        name="dilated",
    )(*qkv, *qkv, *qkv, bias_tabs, ones)


def _softplus(z):
    return jnp.maximum(z, 0.0) + jnp.log(1.0 + jnp.exp2(jnp.abs(z) * (-LOG2E)))


def _sb_kernel(q_ref, k_ref, v_ref, tri_ref, heads_ref, o_ref, qs_ref, acc_ref, carry_ref, live_ref):
    step_id = pl.program_id(2)
    bq = SB_BLOCK
    for s in range(SB_SUBS):
        q = q_ref[s * bq:(s + 1) * bq, :]
        qs_ref[s] = jnp.concatenate([q, q], axis=0) * heads_ref[...]
    tri = tri_ref[...]

    def step(qs, j, carry):
        diagonal = carry is None
        ks = pl.multiple_of(j * bq, bq)
        kb = k_ref[pl.ds(ks, bq), :]
        vb = v_ref[pl.ds(ks, bq), :]
        z = lax.dot_general(qs, kb, _NT, preferred_element_type=F32)
        t = _softplus(z)
        if diagonal:
            row = lax.broadcasted_iota(jnp.int32, z.shape, 0) & (bq - 1)
            colv = lax.broadcasted_iota(jnp.int32, z.shape, 1)
            causal = colv < row
            t = jnp.where(causal, t, 0.0)
        after = jnp.dot(t.astype(BF16), tri, preferred_element_type=F32)
        expo = (z - t) + after
        if not diagonal:
            expo = expo + jnp.concatenate([carry] * (bq // LANES), axis=1)
        a = jnp.exp2(expo * LOG2E)
        if diagonal:
            a = jnp.where(causal, a, 0.0)
        pv = jnp.dot(a.astype(BF16), vb, preferred_element_type=F32)
        block_sum = jnp.broadcast_to(after[:, 0:1] - t[:, 0:1], (2 * bq, LANES))
        return pv, block_sum

    def run(s, first_block, count, fresh):
        qs = qs_ref[s]
        acc, carry = (None, None) if fresh else (acc_ref[s], carry_ref[s])
        for u in range(count):
            pv, block_sum = step(qs, first_block - u, carry)
            acc = pv if acc is None else acc + pv
            carry = block_sum if carry is None else carry + block_sum
        acc_ref[s], carry_ref[s] = acc, carry
        return jnp.max(carry)

    @pl.when(step_id == 0)
    def _():
        run(0, 0, 1, True)
        live_ref[0] = SB_DEAD
        for s in range(1, SB_SUBS):
            live_ref[s] = run(s, s, 2, True)

    @pl.when(step_id > 0)
    def _():
        for s in range(SB_SUBS):
            live_ref[s] = run(s, step_id * SB_SUBS + s, 2, True)

    def older(s, c):
        def more(state):
            j, live = state
            return (j >= 1) & (live > SB_DEAD)

        def pair(state):
            j, _ = state
            return j - 2, run(s, j, 2, False)

        j, live = lax.while_loop(more, pair, (step_id * SB_SUBS + s - 2, live_ref[s]))

        @pl.when((j == 0) & (live > SB_DEAD))
        def _():
            run(s, 0, 1, False)

        return c

    @pl.when(functools.reduce(jnp.maximum, [live_ref[s] for s in range(SB_SUBS)]) > SB_DEAD)
    def _():
        lax.fori_loop(0, SB_SUBS, older, 0)

    for s in range(SB_SUBS):
        o_ref[s * bq:(s + 1) * bq, :] = _merge_pair(acc_ref[s, 0:bq, :], acc_ref[s, bq:2 * bq, :]).astype(o_ref.dtype)


def _stick_breaking(qkv):
    b, t, _ = qkv.shape
    bq = SB_BLOCK
    rows = SB_SUBS * bq
    assert t % rows == 0
    r = np.arange(bq)
    tri = jnp.asarray(np.where(r[:, None] > r[None, :], -1.0, 0.0), dtype=BF16)
    heads = _head_rows(bq)
    q_spec = pl.BlockSpec((None, rows, LANES), lambda bi, p, i: (bi, i, p))
    k_spec = pl.BlockSpec((None, t, LANES), lambda bi, p, i: (bi, 0, PAIRS_B + p))
    v_spec = pl.BlockSpec((None, t, LANES), lambda bi, p, i: (bi, 0, 2 * PAIRS_B + p))
    const = lambda shape: pl.BlockSpec(shape, lambda bi, p, i: (0, 0))
    return pl.pallas_call(
        _sb_kernel,
        grid=(b, PAIRS_B, t // rows),
        in_specs=[q_spec, k_spec, v_spec, const((bq, bq)), const((2 * bq, LANES))],
        out_specs=q_spec,
        out_shape=jax.ShapeDtypeStruct((b, t, WIDTH_B), BF16),
        scratch_shapes=[pltpu.VMEM((SB_SUBS, 2 * bq, LANES), BF16), pltpu.VMEM((SB_SUBS, 2 * bq, LANES), F32),
                        pltpu.VMEM((SB_SUBS, 2 * bq, LANES), F32), pltpu.SMEM((SB_SUBS,), F32)],
        compiler_params=_params(("parallel", "parallel", "arbitrary")),
        name="stick_breaking",
    )(qkv, qkv, qkv, tri, heads)


def _out_kernel(*refs, normalize, project):
    ya_ref, yb_ref, gate_ref, h_ref, *refs = refs
    h = h_ref[...]
    if normalize:
        g0_ref, b0_ref, *refs = refs
        h = _layer_norm_rows(h, g0_ref[...], b0_ref[...])
    w_ref, b_ref, g_ref, beta_ref, *refs = refs
    if project:
        w_next_ref, *refs = refs
    hn_ref, *refs = refs
    half_gate = gate_ref[...].astype(F32)
    silu = half_gate + half_gate * jnp.tanh(half_gate)
    ya = (ya_ref[...].astype(F32) * silu[:, :WIDTH_A]).astype(BF16)
    yb = (yb_ref[...].astype(F32) * silu[:, WIDTH_A:]).astype(BF16)
    out = (jnp.dot(ya, w_ref[0:WIDTH_A, :], preferred_element_type=F32)
           + jnp.dot(yb, w_ref[WIDTH_A:MIX_WIDTH, :], preferred_element_type=F32)
           + b_ref[...])
    y = _layer_norm_rows(DEEPNORM_ALPHA * h + out, g_ref[...], beta_ref[...])
    hn_ref[...] = y
    if project:
        _project_rows(y.astype(BF16), w_next_ref, refs)


def _out_block(ya, yb, gate, h, w_out_bf16, layer, b_out, g, beta, batch, ln0=None, w_in_bf16=None):
    n = h.shape[0]
    project = w_in_bf16 is not None
    tile = ROW_TILE if project else OUT_ROW_TILE
    half = pl.BlockSpec((tile, WIDTH_A), lambda i: (i, 0))
    full = pl.BlockSpec((tile, D_MODEL), lambda i: (i, 0))
    vec = pl.BlockSpec((1, D_MODEL), lambda i: (0, 0))
    as_vec = lambda a: a.reshape(1, D_MODEL)
    ln_args = () if ln0 is None else tuple(as_vec(a) for a in ln0)
    in_specs = [half, half, full, full] + [vec] * len(ln_args) + [
        pl.BlockSpec((None, MIX_WIDTH, D_MODEL), lambda i: (layer, 0, 0), pipeline_mode=pl.Buffered(1)),
        vec, vec, vec]
    args = [ya.reshape(n, WIDTH_A), yb.reshape(n, WIDTH_B), gate, h, *ln_args, w_out_bf16,
            as_vec(b_out), as_vec(g), as_vec(beta)]
    out_specs, out_shapes, scratch = [full], [jax.ShapeDtypeStruct((n, D_MODEL), F32)], []
    if project:
        weights, proj_specs, proj_shapes, scratch = _projection_plumbing(n, batch, layer + 1)
        in_specs, args = in_specs + [weights], args + [w_in_bf16]
        out_specs, out_shapes = out_specs + proj_specs, out_shapes + proj_shapes
    outs = pl.pallas_call(
        functools.partial(_out_kernel, normalize=ln0 is not None, project=project),
        grid=(n // tile,),
        in_specs=in_specs,
        out_specs=out_specs,
        out_shape=out_shapes,
        scratch_shapes=scratch,
        compiler_params=_params(("parallel",)),
        name="out_proj_norm",
    )(*args)
    if not project:
        return outs[0], None
    return outs[0], (outs[1:1 + N_CFG], outs[1 + N_CFG], outs[2 + N_CFG])


def kernel(x, ln0_g, ln0_b, w_in, w_out, b_out, ln_g, ln_b, rel_bias):
    b, t, _ = x.shape
    n = b * t
    bias_tabs = _bias_tables(rel_bias)
    w_in, w_out = w_in.astype(BF16), w_out.astype(BF16)
    h = x.reshape(n, D_MODEL)
    ln0 = (ln0_g, ln0_b)
    projections = _project(h, w_in, 0, b, ln0)
    for l in range(DEPTH):
        qkv_a, qkv_b, gate = projections
        ya = _dilated(qkv_a, bias_tabs)
        yb = _stick_breaking(qkv_b.reshape(b, t, 3 * WIDTH_B))
        h, projections = _out_block(ya, yb, gate, h, w_out, l, b_out[l], ln_g[l], ln_b[l], b, ln0=ln0,
                                    w_in_bf16=w_in if l + 1 < DEPTH else None)
        ln0 = None
    return h.reshape(b, t, D_MODEL)
```

```python
import functools
import math

import numpy as np
import jax
import jax.numpy as jnp
from jax import lax
from jax.experimental import pallas as pl
from jax.experimental.pallas import tpu as pltpu

D_MODEL = 1024
DEPTH = 2
HEAD_DIM = 64
N_HEADS_A = 8
N_HEADS_B = 8
WIDTH_A = N_HEADS_A * HEAD_DIM
WIDTH_B = N_HEADS_B * HEAD_DIM
MIX_WIDTH = WIDTH_A + WIDTH_B
IN_COLS = 3 * WIDTH_A + 3 * WIDTH_B + MIX_WIDTH
DILATED_CONFIGS = ((128, 1), (512, 4), (2048, 16))
N_CFG = len(DILATED_CONFIGS)
BLK = 128
N_BUCKETS = 32
MAX_DISTANCE = 2048
LN_EPS = 1e-5
DEEPNORM_ALPHA = (2.0 * DEPTH) ** 0.25
QK_SCALE = 1.0 / math.sqrt(HEAD_DIM)
LOG2E = math.log2(math.e)

LANES = 128
PAIRS_A = WIDTH_A // LANES
PAIRS_B = WIDTH_B // LANES
NEG_BIG = -1e30

ROW_TILE = 512
OUT_ROW_TILE = 1024
SB_BLOCK = 256
SB_SUBS = 16
SB_DEAD = -110.0
DIL_CHUNK = 2 * BLK * max(d for _, d in DILATED_CONFIGS)
DIL_BLOCKS = DIL_CHUNK // BLK
VMEM_LIMIT = 48 * 1024 * 1024

F32 = jnp.float32
BF16 = jnp.bfloat16


def _params(semantics):
    return pltpu.CompilerParams(dimension_semantics=semantics, vmem_limit_bytes=VMEM_LIMIT)


def _layer_norm_rows(x, g, b):
    mu = jnp.mean(x, axis=-1, keepdims=True)
    xc = x - mu
    var = jnp.mean(xc * xc, axis=-1, keepdims=True)
    return xc * lax.rsqrt(var + LN_EPS) * g + b


_PROJ_A = ((0, QK_SCALE * LOG2E), (WIDTH_A, 1.0), (2 * WIDTH_A, 1.0))
_PROJ_B = ((3 * WIDTH_A, QK_SCALE), (3 * WIDTH_A + WIDTH_B, 1.0), (3 * WIDTH_A + 2 * WIDTH_B, 1.0))
_GATE_COL = 3 * WIDTH_A + 3 * WIDTH_B
_DILS = tuple(d for _, d in DILATED_CONFIGS)


def _proj_kernel(*refs, normalize):
    if normalize:
        x_ref, g_ref, b_ref, w_ref, *refs = refs
        h = _layer_norm_rows(x_ref[...], g_ref[...], b_ref[...]).astype(BF16)
    else:
        h_ref, w_ref, *refs = refs
        h = h_ref[...]
    _project_rows(h, w_ref, refs)


def _project_rows(h, w_ref, refs):
    a_refs, (b_ref, gate_ref), fold_refs = refs[:N_CFG], refs[N_CFG:N_CFG + 2], refs[N_CFG + 2:]
    slabs = range(WIDTH_A // LANES)
    for t, (c0, scale) in enumerate(_PROJ_A):
        cols = slice(t * WIDTH_A, (t + 1) * WIDTH_A)
        acc = jnp.dot(h, w_ref[:, c0:c0 + WIDTH_A], preferred_element_type=F32)
        if scale != 1.0:
            acc = acc * scale
        a_refs[0][0, :, cols] = acc.astype(BF16)
        src_ref, src_dil = fold_refs[2 * t], 1
        for j in slabs:
            src_ref[j] = acc[:, j * LANES:(j + 1) * LANES]
        for cfg in range(1, N_CFG):
            dil, dst_ref = _DILS[cfg], fold_refs[2 * t + cfg % 2]
            step, rows_out = dil // src_dil, ROW_TILE // dil
            for r_src in range(src_dil):
                for a in range(step):
                    r = r_src + src_dil * a
                    rows = pl.ds(r_src * (ROW_TILE // src_dil) + a, rows_out, stride=step)
                    parts = [src_ref[j, rows, :] for j in slabs]
                    a_refs[cfg][r, :, cols] = jnp.concatenate(parts, axis=1).astype(BF16)
                    if cfg + 1 < N_CFG:
                        for j in slabs:
                            dst_ref[j, r * rows_out:(r + 1) * rows_out, :] = parts[j]
            src_ref, src_dil = dst_ref, dil
    for t, (c0, scale) in enumerate(_PROJ_B):
        acc = jnp.dot(h, w_ref[:, c0:c0 + WIDTH_B], preferred_element_type=F32)
        if scale != 1.0:
            acc = acc * scale
        b_ref[:, t * WIDTH_B:(t + 1) * WIDTH_B] = acc.astype(BF16)
    gate = jnp.dot(h, w_ref[:, _GATE_COL:_GATE_COL + MIX_WIDTH], preferred_element_type=F32)
    gate_ref[...] = (0.5 * gate).astype(BF16)


def _projection_plumbing(n, batch, layer):
    t = n // batch
    tiles = t // ROW_TILE
    assert t % ROW_TILE == 0 and all(ROW_TILE % (16 * d) == 0 for d in _DILS)
    weights = pl.BlockSpec((None, D_MODEL, IN_COLS), lambda i: (layer, 0, 0),
                           pipeline_mode=pl.Buffered(1))
    a_specs = [pl.BlockSpec((None, d, ROW_TILE // d, 3 * WIDTH_A), lambda i: (i // tiles, 0, i % tiles, 0))
               for d in _DILS]
    a_shapes = [jax.ShapeDtypeStruct((batch, d, t // d, 3 * WIDTH_A), BF16) for d in _DILS]
    b_widths = (3 * WIDTH_B, MIX_WIDTH)
    out_specs = a_specs + [pl.BlockSpec((ROW_TILE, w), lambda i: (i, 0)) for w in b_widths]
    out_shapes = a_shapes + [jax.ShapeDtypeStruct((n, w), BF16) for w in b_widths]
    scratch = [pltpu.VMEM((WIDTH_A // LANES, ROW_TILE, LANES), F32)] * (2 * len(_PROJ_A))
    return weights, out_specs, out_shapes, scratch


def _project(rows, w_bf16, layer, batch, ln0=None):
    n = rows.shape[0]
    row = pl.BlockSpec((ROW_TILE, D_MODEL), lambda i: (i, 0))
    vec = pl.BlockSpec((1, D_MODEL), lambda i: (0, 0))
    ln_args = () if ln0 is None else tuple(a.reshape(1, D_MODEL) for a in ln0)
    weights, out_specs, out_shapes, scratch = _projection_plumbing(n, batch, layer)
    outs = pl.pallas_call(
        functools.partial(_proj_kernel, normalize=ln0 is not None),
        grid=(n // ROW_TILE,),
        in_specs=[row] + [vec] * len(ln_args) + [weights],
        out_specs=out_specs,
        out_shape=out_shapes,
        scratch_shapes=scratch,
        compiler_params=_params(("parallel",)),
        name="in_proj",
    )(rows, *ln_args, w_bf16)
    return outs[:N_CFG], outs[N_CFG], outs[N_CFG + 1]


def _head_rows(rows_per_head):
    head_of_lane = np.arange(LANES) // HEAD_DIM
    head_of_row = np.arange(2 * rows_per_head) // rows_per_head
    return jnp.asarray(head_of_row[:, None] == head_of_lane[None, :], dtype=BF16)


def _merge_pair(x0, x1):
    lane = lax.broadcasted_iota(jnp.int32, x0.shape, 1)
    return jnp.where(lane < HEAD_DIM, x0, x1)


_NT = (((1,), (1,)), ((), ()))


def _t5_bucket(dist):
    max_exact = N_BUCKETS // 2
    n = np.maximum(dist, 1).astype(np.float32)
    large = max_exact + (np.log(n / max_exact) / math.log(MAX_DISTANCE / max_exact)
                         * (N_BUCKETS - max_exact)).astype(np.int32)
    large = np.minimum(large, N_BUCKETS - 1)
    return np.where(dist < max_exact, dist, large).astype(np.int32)


def _bucket_segments(window, dil):
    buckets = _t5_bucket(np.arange(window // dil + 1) * dil)
    return [(d, int(bk)) for d, bk in enumerate(buckets) if d == 0 or bk != buckets[d - 1]]


def _bias_kernel(rel_ref, out_ref):
    i = lax.broadcasted_iota(jnp.int32, (BLK, 2 * BLK), 0)
    j = lax.broadcasted_iota(jnp.int32, (BLK, 2 * BLK), 1)
    delta = BLK + i - j
    for cfg, (window, dil) in enumerate(DILATED_CONFIGS):
        segments = _bucket_segments(window, dil)
        for variant in range(2):
            band = (delta >= 0) & (delta <= window // dil) & (j >= variant * BLK)
            steps = [delta >= start for start, _ in segments[1:]]
            for h in range(N_HEADS_A):
                val = jnp.full((BLK, 2 * BLK), rel_ref[segments[0][1], h] * LOG2E, F32)
                for step, (_, bucket) in zip(steps, segments[1:]):
                    val = jnp.where(step, rel_ref[bucket, h] * LOG2E, val)
                out_ref[cfg, variant, h] = jnp.where(band, val, NEG_BIG)


def _bias_tables(rel_bias):
    shape = (N_CFG, 2, N_HEADS_A, BLK, 2 * BLK)
    return pl.pallas_call(
        _bias_kernel,
        in_specs=[pl.BlockSpec(memory_space=pltpu.SMEM)],
        out_shape=jax.ShapeDtypeStruct(shape, F32),
        compiler_params=pltpu.CompilerParams(vmem_limit_bytes=VMEM_LIMIT),
        name="bias_tables",
    )(rel_bias.astype(F32))


def _dilated_kernel(*refs):
    q_refs, k_refs, v_refs = refs[:N_CFG], refs[N_CFG:2 * N_CFG], refs[2 * N_CFG:3 * N_CFG]
    bias_ref, ones_ref, o_ref, *scratch_refs = refs[3 * N_CFG:]
    prev_k_refs, prev_v_refs, state_refs = scratch_refs[:N_CFG], scratch_refs[N_CFG:2 * N_CFG], scratch_refs[2 * N_CFG:]
    c = pl.program_id(2)
    n_state = N_CFG - 1
    m_refs, den_refs, num_refs = state_refs[:n_state], state_refs[n_state:2 * n_state], state_refs[2 * n_state:]

    @pl.when(c == 0)
    def _():
        for cfg in range(N_CFG):
            prev_k_refs[cfg][...] = k_refs[cfg][:, 0:BLK, :]
            prev_v_refs[cfg][...] = v_refs[cfg][:, 0:BLK, :]

    def key_rows(ref, prev_ref, r, n):
        if n == 0:
            return jnp.concatenate([prev_ref[r], ref[r, 0:BLK, :]], axis=0)
        return ref[r, (n - 1) * BLK:(n + 1) * BLK, :]

    def attend(cfg, dil, r, n):
        first = jnp.where(c == 0, 1, 0) if n == 0 else 0
        ones = ones_ref[...]
        q = q_refs[cfg][r, n * BLK:(n + 1) * BLK, :]
        q_pair = (q * ones[0:BLK], q * ones[2 * BLK:3 * BLK])
        kb = key_rows(k_refs[cfg], prev_k_refs[cfg], r, n)
        vb = key_rows(v_refs[cfg], prev_v_refs[cfg], r, n)
        v_sum = jnp.concatenate([jnp.concatenate([vb, vb], axis=0) * ones, ones], axis=1)
        ms, ps = [], []
        for h in range(2):
            s = lax.dot_general(q_pair[h], kb, _NT, preferred_element_type=F32) + bias_ref[cfg, first, h]
            m = jnp.max(s, axis=-1, keepdims=True)
            ps.append(jnp.exp2(s - m).astype(BF16))
            ms.append(jnp.broadcast_to(m, (BLK, LANES)))
        both = jnp.dot(jnp.concatenate(ps, axis=1), v_sum, preferred_element_type=F32)
        m, num, den = _merge_pair(*ms), both[:, :LANES], both[:, LANES:]
        if cfg > 0:
            rows = pl.ds(n * BLK * dil + r, BLK, stride=dil)
            m_refs[cfg - 1][rows, :], num_refs[cfg - 1][rows, :], den_refs[cfg - 1][rows, :] = m, num, den
            return
        rows = pl.ds(n * BLK, BLK)
        others = [ref[rows, :] for ref in m_refs]
        top = functools.reduce(jnp.maximum, others, m)
        w = jnp.exp2(m - top)
        num, den = w * num, w * den
        for m_c, num_ref, den_ref in zip(others, num_refs, den_refs):
            w = jnp.exp2(m_c - top)
            num, den = num + w * num_ref[rows, :], den + w * den_ref[rows, :]
        o_ref[rows, :] = (num / den).astype(o_ref.dtype)

    assert _DILS[0] == 1
    for cfg in reversed(range(N_CFG)):
        dil = _DILS[cfg]
        for r in range(dil):
            for n in range(DIL_BLOCKS // dil):
                attend(cfg, dil, r, n)
    for cfg, dil in enumerate(_DILS):
        last = DIL_CHUNK // dil - BLK
        prev_k_refs[cfg][...] = k_refs[cfg][:, last:last + BLK, :]
        prev_v_refs[cfg][...] = v_refs[cfg][:, last:last + BLK, :]


def _dilated(qkv, bias_tabs):
    b, _, t, _ = qkv[0].shape
    assert t % DIL_CHUNK == 0
    chunk = lambda d, first_col: pl.BlockSpec((None, d, DIL_CHUNK // d, LANES),
                                              lambda bi, p, c: (bi, 0, c, first_col + p))
    q_specs = [chunk(d, 0) for d in _DILS]
    k_specs = [chunk(d, PAIRS_A) for d in _DILS]
    v_specs = [chunk(d, 2 * PAIRS_A) for d in _DILS]
    carried = [pltpu.VMEM((d, BLK, LANES), BF16) for d in _DILS]
    bias_spec = pl.BlockSpec((N_CFG, 2, 2, BLK, 2 * BLK), lambda bi, p, c: (0, 0, p, 0, 0))
    ones = _head_rows(2 * BLK)
    return pl.pallas_call(
        _dilated_kernel,
        grid=(b, PAIRS_A, t // DIL_CHUNK),
        in_specs=q_specs + k_specs + v_specs
                 + [bias_spec, pl.BlockSpec((4 * BLK, LANES), lambda bi, p, c: (0, 0))],
        out_specs=pl.BlockSpec((None, DIL_CHUNK, LANES), lambda bi, p, c: (bi, c, p)),
        out_shape=jax.ShapeDtypeStruct((b, t, WIDTH_A), BF16),
        scratch_shapes=carried + carried + [pltpu.VMEM((DIL_CHUNK, LANES), F32)] * (3 * (N_CFG - 1)),
        compiler_params=_params(("parallel", "parallel", "arbitrary")),
        name="dilated",
    )(*qkv, *qkv, *qkv, bias_tabs, ones)


def _softplus(z):
    return jnp.maximum(z, 0.0) + jnp.log(1.0 + jnp.exp2(jnp.abs(z) * (-LOG2E)))


def _sb_kernel(q_ref, k_ref, v_ref, tri_ref, heads_ref, o_ref, qs_ref, acc_ref, carry_ref, live_ref):
    step_id = pl.program_id(2)
    bq = SB_BLOCK
    for s in range(SB_SUBS):
        q = q_ref[s * bq:(s + 1) * bq, :]
        qs_ref[s] = jnp.concatenate([q, q], axis=0) * heads_ref[...]
    tri = tri_ref[...]

    def step(qs, j, carry):
        diagonal = carry is None
        ks = pl.multiple_of(j * bq, bq)
        kb = k_ref[pl.ds(ks, bq), :]
        vb = v_ref[pl.ds(ks, bq), :]
        z = lax.dot_general(qs, kb, _NT, preferred_element_type=F32)
        t = _softplus(z)
        if diagonal:
            row = lax.broadcasted_iota(jnp.int32, z.shape, 0) & (bq - 1)
            colv = lax.broadcasted_iota(jnp.int32, z.shape, 1)
            causal = colv < row
            t = jnp.where(causal, t, 0.0)
        after = jnp.dot(t.astype(BF16), tri, preferred_element_type=F32)
        expo = (z - t) + after
        if not diagonal:
            expo = expo + jnp.concatenate([carry] * (bq // LANES), axis=1)
        a = jnp.exp2(expo * LOG2E)
        if diagonal:
            a = jnp.where(causal, a, 0.0)
        pv = jnp.dot(a.astype(BF16), vb, preferred_element_type=F32)
        block_sum = jnp.broadcast_to(after[:, 0:1] - t[:, 0:1], (2 * bq, LANES))
        return pv, block_sum

    def run(s, first_block, count, fresh):
        qs = qs_ref[s]
        acc, carry = (None, None) if fresh else (acc_ref[s], carry_ref[s])
        for u in range(count):
            pv, block_sum = step(qs, first_block - u, carry)
            acc = pv if acc is None else acc + pv
            carry = block_sum if carry is None else carry + block_sum
        acc_ref[s], carry_ref[s] = acc, carry
        return jnp.max(carry)

    @pl.when(step_id == 0)
    def _():
        run(0, 0, 1, True)
        live_ref[0] = SB_DEAD
        for s in range(1, SB_SUBS):
            live_ref[s] = run(s, s, 2, True)

    @pl.when(step_id > 0)
    def _():
        for s in range(SB_SUBS):
            live_ref[s] = run(s, step_id * SB_SUBS + s, 2, True)

    def older(s, c):
        def more(state):
            j, live = state
            return (j >= 1) & (live > SB_DEAD)

        def pair(state):
            j, _ = state
            return j - 2, run(s, j, 2, False)

        j, live = lax.while_loop(more, pair, (step_id * SB_SUBS + s - 2, live_ref[s]))

        @pl.when((j == 0) & (live > SB_DEAD))
        def _():
            run(s, 0, 1, False)

        return c

    @pl.when(functools.reduce(jnp.maximum, [live_ref[s] for s in range(SB_SUBS)]) > SB_DEAD)
    def _():
        lax.fori_loop(0, SB_SUBS, older, 0)

    for s in range(SB_SUBS):
        o_ref[s * bq:(s + 1) * bq, :] = _merge_pair(acc_ref[s, 0:bq, :], acc_ref[s, bq:2 * bq, :]).astype(o_ref.dtype)


def _stick_breaking(qkv):
    b, t, _ = qkv.shape
    bq = SB_BLOCK
    rows = SB_SUBS * bq
    assert t % rows == 0
    r = np.arange(bq)
    tri = jnp.asarray(np.where(r[:, None] > r[None, :], -1.0, 0.0), dtype=BF16)
    heads = _head_rows(bq)
    q_spec = pl.BlockSpec((None, rows, LANES), lambda bi, p, i: (bi, i, p))
    k_spec = pl.BlockSpec((None, t, LANES), lambda bi, p, i: (bi, 0, PAIRS_B + p))
    v_spec = pl.BlockSpec((None, t, LANES), lambda bi, p, i: (bi, 0, 2 * PAIRS_B + p))
    const = lambda shape: pl.BlockSpec(shape, lambda bi, p, i: (0, 0))
    return pl.pallas_call(
        _sb_kernel,
        grid=(b, PAIRS_B, t // rows),
        in_specs=[q_spec, k_spec, v_spec, const((bq, bq)), const((2 * bq, LANES))],
        out_specs=q_spec,
        out_shape=jax.ShapeDtypeStruct((b, t, WIDTH_B), BF16),
        scratch_shapes=[pltpu.VMEM((SB_SUBS, 2 * bq, LANES), BF16), pltpu.VMEM((SB_SUBS, 2 * bq, LANES), F32),
                        pltpu.VMEM((SB_SUBS, 2 * bq, LANES), F32), pltpu.SMEM((SB_SUBS,), F32)],
        compiler_params=_params(("parallel", "parallel", "arbitrary")),
        name="stick_breaking",
    )(qkv, qkv, qkv, tri, heads)


def _out_kernel(*refs, normalize, project):
    ya_ref, yb_ref, gate_ref, h_ref, *refs = refs
    h = h_ref[...]
    if normalize:
        g0_ref, b0_ref, *refs = refs
        h = _layer_norm_rows(h, g0_ref[...], b0_ref[...])
    w_ref, b_ref, g_ref, beta_ref, *refs = refs
    if project:
        w_next_ref, *refs = refs
    hn_ref, *refs = refs
    half_gate = gate_ref[...].astype(F32)
    silu = half_gate + half_gate * jnp.tanh(half_gate)
    ya = (ya_ref[...].astype(F32) * silu[:, :WIDTH_A]).astype(BF16)
    yb = (yb_ref[...].astype(F32) * silu[:, WIDTH_A:]).astype(BF16)
    out = (jnp.dot(ya, w_ref[0:WIDTH_A, :], preferred_element_type=F32)
           + jnp.dot(yb, w_ref[WIDTH_A:MIX_WIDTH, :], preferred_element_type=F32)
           + b_ref[...])
    y = _layer_norm_rows(DEEPNORM_ALPHA * h + out, g_ref[...], beta_ref[...])
    hn_ref[...] = y
    if project:
        _project_rows(y.astype(BF16), w_next_ref, refs)


def _out_block(ya, yb, gate, h, w_out_bf16, layer, b_out, g, beta, batch, ln0=None, w_in_bf16=None):
    n = h.shape[0]
    project = w_in_bf16 is not None
    tile = ROW_TILE if project else OUT_ROW_TILE
    half = pl.BlockSpec((tile, WIDTH_A), lambda i: (i, 0))
    full = pl.BlockSpec((tile, D_MODEL), lambda i: (i, 0))
    vec = pl.BlockSpec((1, D_MODEL), lambda i: (0, 0))
    as_vec = lambda a: a.reshape(1, D_MODEL)
    ln_args = () if ln0 is None else tuple(as_vec(a) for a in ln0)
    in_specs = [half, half, full, full] + [vec] * len(ln_args) + [
        pl.BlockSpec((None, MIX_WIDTH, D_MODEL), lambda i: (layer, 0, 0), pipeline_mode=pl.Buffered(1)),
        vec, vec, vec]
    args = [ya.reshape(n, WIDTH_A), yb.reshape(n, WIDTH_B), gate, h, *ln_args, w_out_bf16,
            as_vec(b_out), as_vec(g), as_vec(beta)]
    out_specs, out_shapes, scratch = [full], [jax.ShapeDtypeStruct((n, D_MODEL), F32)], []
    if project:
        weights, proj_specs, proj_shapes, scratch = _projection_plumbing(n, batch, layer + 1)
        in_specs, args = in_specs + [weights], args + [w_in_bf16]
        out_specs, out_shapes = out_specs + proj_specs, out_shapes + proj_shapes
    outs = pl.pallas_call(
        functools.partial(_out_kernel, normalize=ln0 is not None, project=project),
        grid=(n // tile,),
        in_specs=in_specs,
        out_specs=out_specs,
        out_shape=out_shapes,
        scratch_shapes=scratch,
        compiler_params=_params(("parallel",)),
        name="out_proj_norm",
    )(*args)
    if not project:
        return outs[0], None
    return outs[0], (outs[1:1 + N_CFG], outs[1 + N_CFG], outs[2 + N_CFG])


def kernel(x, ln0_g, ln0_b, w_in, w_out, b_out, ln_g, ln_b, rel_bias):
    b, t, _ = x.shape
    n = b * t
    bias_tabs = _bias_tables(rel_bias)
    w_in, w_out = w_in.astype(BF16), w_out.astype(BF16)
    h = x.reshape(n, D_MODEL)
    ln0 = (ln0_g, ln0_b)
    projections = _project(h, w_in, 0, b, ln0)
    for l in range(DEPTH):
        qkv_a, qkv_b, gate = projections
        yb = _stick_breaking(qkv_b.reshape(b, t, 3 * WIDTH_B))
        ya = _dilated(qkv_a, bias_tabs)
        h, projections = _out_block(ya, yb, gate, h, w_out, l, b_out[l], ln_g[l], ln_b[l], b, ln0=ln0,
                                    w_in_bf16=w_in if l + 1 < DEPTH else None)
        ln0 = None
    return h.reshape(b, t, D_MODEL)
```

```python
import functools
import math

import numpy as np
import jax
import jax.numpy as jnp
from jax import lax
from jax.experimental import pallas as pl
from jax.experimental.pallas import tpu as pltpu

D_MODEL = 1024
DEPTH = 2
HEAD_DIM = 64
N_HEADS_A = 8
N_HEADS_B = 8
WIDTH_A = N_HEADS_A * HEAD_DIM
WIDTH_B = N_HEADS_B * HEAD_DIM
MIX_WIDTH = WIDTH_A + WIDTH_B
IN_COLS = 3 * WIDTH_A + 3 * WIDTH_B + MIX_WIDTH
DILATED_CONFIGS = ((128, 1), (512, 4), (2048, 16))
N_CFG = len(DILATED_CONFIGS)
BLK = 128
N_BUCKETS = 32
MAX_DISTANCE = 2048
LN_EPS = 1e-5
DEEPNORM_ALPHA = (2.0 * DEPTH) ** 0.25
QK_SCALE = 1.0 / math.sqrt(HEAD_DIM)
LOG2E = math.log2(math.e)

LANES = 128
BF16_ROWS = 16
PAIRS_A = WIDTH_A // LANES
PAIRS_B = WIDTH_B // LANES
NEG_BIG = -1e30

ROW_TILE = 512
OUT_ROW_TILE = 1024
SB_BLOCK = 256
SB_SUBS = 16
SB_DEAD = -110.0
DIL_CHUNK = 2 * BLK * max(d for _, d in DILATED_CONFIGS)
DIL_BLOCKS = DIL_CHUNK // BLK
VMEM_LIMIT = 48 * 1024 * 1024

F32 = jnp.float32
BF16 = jnp.bfloat16


def _params(semantics):
    return pltpu.CompilerParams(dimension_semantics=semantics, vmem_limit_bytes=VMEM_LIMIT)


def _layer_norm_rows(x, g, b):
    mu = jnp.mean(x, axis=-1, keepdims=True)
    xc = x - mu
    var = jnp.mean(xc * xc, axis=-1, keepdims=True)
    return xc * lax.rsqrt(var + LN_EPS) * g + b


_PROJ_A = ((0, QK_SCALE * LOG2E), (WIDTH_A, 1.0), (2 * WIDTH_A, 1.0))
_PROJ_B = ((3 * WIDTH_A, QK_SCALE), (3 * WIDTH_A + WIDTH_B, 1.0), (3 * WIDTH_A + 2 * WIDTH_B, 1.0))
_GATE_COL = 3 * WIDTH_A + 3 * WIDTH_B
_DILS = tuple(d for _, d in DILATED_CONFIGS)


def _proj_kernel(x_ref, g_ref, b_ref, w_ref, *refs):
    _project_rows(_layer_norm_rows(x_ref[...], g_ref[...], b_ref[...]).astype(BF16), w_ref, refs)


def _project_rows(h, w_ref, refs):
    a_refs, (b_ref, gate_ref), fold_refs = refs[:N_CFG], refs[N_CFG:N_CFG + 2], refs[N_CFG + 2:]
    slabs = range(WIDTH_A // LANES)
    for t, (c0, scale) in enumerate(_PROJ_A):
        cols = slice(t * WIDTH_A, (t + 1) * WIDTH_A)
        acc = jnp.dot(h, w_ref[:, c0:c0 + WIDTH_A], preferred_element_type=F32)
        if scale != 1.0:
            acc = acc * scale
        a_refs[0][0, :, cols] = acc.astype(BF16)
        src_ref, src_dil = fold_refs[2 * t], 1
        for j in slabs:
            src_ref[j] = acc[:, j * LANES:(j + 1) * LANES]
        for cfg in range(1, N_CFG):
            dil, dst_ref = _DILS[cfg], fold_refs[2 * t + cfg % 2]
            step, rows_out = dil // src_dil, ROW_TILE // dil
            for r_src in range(src_dil):
                for a in range(step):
                    r = r_src + src_dil * a
                    rows = pl.ds(r_src * (ROW_TILE // src_dil) + a, rows_out, stride=step)
                    parts = [src_ref[j, rows, :] for j in slabs]
                    a_refs[cfg][r, :, cols] = jnp.concatenate(parts, axis=1).astype(BF16)
                    if cfg + 1 < N_CFG:
                        for j in slabs:
                            dst_ref[j, r * rows_out:(r + 1) * rows_out, :] = parts[j]
            src_ref, src_dil = dst_ref, dil
    for t, (c0, scale) in enumerate(_PROJ_B):
        acc = jnp.dot(h, w_ref[:, c0:c0 + WIDTH_B], preferred_element_type=F32)
        if scale != 1.0:
            acc = acc * scale
        b_ref[:, t * WIDTH_B:(t + 1) * WIDTH_B] = acc.astype(BF16)
    gate = jnp.dot(h, w_ref[:, _GATE_COL:_GATE_COL + MIX_WIDTH], preferred_element_type=F32)
    gate_ref[...] = (0.5 * gate).astype(BF16)


def _projection_plumbing(n, batch, layer):
    t = n // batch
    tiles = t // ROW_TILE
    assert t % ROW_TILE == 0 and all(ROW_TILE % (BF16_ROWS * d) == 0 for d in _DILS)
    weights = pl.BlockSpec((None, D_MODEL, IN_COLS), lambda i: (layer, 0, 0),
                           pipeline_mode=pl.Buffered(1))
    a_specs = [pl.BlockSpec((None, d, ROW_TILE // d, 3 * WIDTH_A), lambda i: (i // tiles, 0, i % tiles, 0))
               for d in _DILS]
    a_shapes = [jax.ShapeDtypeStruct((batch, d, t // d, 3 * WIDTH_A), BF16) for d in _DILS]
    b_widths = (3 * WIDTH_B, MIX_WIDTH)
    out_specs = a_specs + [pl.BlockSpec((ROW_TILE, w), lambda i: (i, 0)) for w in b_widths]
    out_shapes = a_shapes + [jax.ShapeDtypeStruct((n, w), BF16) for w in b_widths]
    scratch = [pltpu.VMEM((WIDTH_A // LANES, ROW_TILE, LANES), F32)] * (2 * len(_PROJ_A))
    return weights, out_specs, out_shapes, scratch


def _project(x2, w_bf16, batch, ln0):
    n = x2.shape[0]
    row = pl.BlockSpec((ROW_TILE, D_MODEL), lambda i: (i, 0))
    vec = pl.BlockSpec((1, D_MODEL), lambda i: (0, 0))
    weights, out_specs, out_shapes, scratch = _projection_plumbing(n, batch, 0)
    outs = pl.pallas_call(
        _proj_kernel,
        grid=(n // ROW_TILE,),
        in_specs=[row, vec, vec, weights],
        out_specs=out_specs,
        out_shape=out_shapes,
        scratch_shapes=scratch,
        compiler_params=_params(("parallel",)),
        name="in_proj",
    )(x2, *(a.reshape(1, D_MODEL) for a in ln0), w_bf16)
    return outs[:N_CFG], outs[N_CFG], outs[N_CFG + 1]


def _head_rows(rows_per_head):
    head_of_lane = np.arange(LANES) // HEAD_DIM
    head_of_row = np.arange(2 * rows_per_head) // rows_per_head
    return jnp.asarray(head_of_row[:, None] == head_of_lane[None, :], dtype=BF16)


def _merge_pair(x0, x1):
    lane = lax.broadcasted_iota(jnp.int32, x0.shape, 1)
    return jnp.where(lane < HEAD_DIM, x0, x1)


_NT = (((1,), (1,)), ((), ()))


def _t5_bucket(dist):
    max_exact = N_BUCKETS // 2
    n = np.maximum(dist, 1).astype(np.float32)
    large = max_exact + (np.log(n / max_exact) / math.log(MAX_DISTANCE / max_exact)
                         * (N_BUCKETS - max_exact)).astype(np.int32)
    large = np.minimum(large, N_BUCKETS - 1)
    return np.where(dist < max_exact, dist, large).astype(np.int32)


def _bucket_segments(window, dil):
    buckets = _t5_bucket(np.arange(window // dil + 1) * dil)
    return [(d, int(bk)) for d, bk in enumerate(buckets) if d == 0 or bk != buckets[d - 1]]


def _bias_kernel(rel_ref, out_ref):
    i = lax.broadcasted_iota(jnp.int32, (BLK, 2 * BLK), 0)
    j = lax.broadcasted_iota(jnp.int32, (BLK, 2 * BLK), 1)
    delta = BLK + i - j
    for cfg, (window, dil) in enumerate(DILATED_CONFIGS):
        segments = _bucket_segments(window, dil)
        for variant in range(2):
            band = (delta >= 0) & (delta <= window // dil) & (j >= variant * BLK)
            steps = [delta >= start for start, _ in segments[1:]]
            for h in range(N_HEADS_A):
                val = jnp.full((BLK, 2 * BLK), rel_ref[segments[0][1], h] * LOG2E, F32)
                for step, (_, bucket) in zip(steps, segments[1:]):
                    val = jnp.where(step, rel_ref[bucket, h] * LOG2E, val)
                out_ref[cfg, variant, h] = jnp.where(band, val, NEG_BIG)


def _bias_tables(rel_bias):
    shape = (N_CFG, 2, N_HEADS_A, BLK, 2 * BLK)
    return pl.pallas_call(
        _bias_kernel,
        in_specs=[pl.BlockSpec(memory_space=pltpu.SMEM)],
        out_shape=jax.ShapeDtypeStruct(shape, F32),
        compiler_params=pltpu.CompilerParams(vmem_limit_bytes=VMEM_LIMIT),
        name="bias_tables",
    )(rel_bias.astype(F32))


def _dilated_kernel(*refs):
    q_refs, k_refs, v_refs = refs[:N_CFG], refs[N_CFG:2 * N_CFG], refs[2 * N_CFG:3 * N_CFG]
    bias_ref, ones_ref, o_ref, *scratch_refs = refs[3 * N_CFG:]
    prev_k_refs, prev_v_refs, state_refs = scratch_refs[:N_CFG], scratch_refs[N_CFG:2 * N_CFG], scratch_refs[2 * N_CFG:]
    c = pl.program_id(2)
    n_state = N_CFG - 1
    m_refs, den_refs, num_refs = state_refs[:n_state], state_refs[n_state:2 * n_state], state_refs[2 * n_state:]

    @pl.when(c == 0)
    def _():
        for cfg in range(N_CFG):
            prev_k_refs[cfg][...] = k_refs[cfg][:, 0:BLK, :]
            prev_v_refs[cfg][...] = v_refs[cfg][:, 0:BLK, :]

    def key_rows(ref, prev_ref, r, n):
        if n == 0:
            return jnp.concatenate([prev_ref[r], ref[r, 0:BLK, :]], axis=0)
        return ref[r, (n - 1) * BLK:(n + 1) * BLK, :]

    def attend(cfg, dil, r, n):
        first = jnp.where(c == 0, 1, 0) if n == 0 else 0
        ones = ones_ref[...]
        q = q_refs[cfg][r, n * BLK:(n + 1) * BLK, :]
        q_pair = (q * ones[0:BLK], q * ones[2 * BLK:3 * BLK])
        kb = key_rows(k_refs[cfg], prev_k_refs[cfg], r, n)
        vb = key_rows(v_refs[cfg], prev_v_refs[cfg], r, n)
        v_sum = jnp.concatenate([jnp.concatenate([vb, vb], axis=0) * ones, ones], axis=1)
        ms, ps = [], []
        for h in range(2):
            s = lax.dot_general(q_pair[h], kb, _NT, preferred_element_type=F32) + bias_ref[cfg, first, h]
            m = jnp.max(s, axis=-1, keepdims=True)
            ps.append(jnp.exp2(s - m).astype(BF16))
            ms.append(jnp.broadcast_to(m, (BLK, LANES)))
        both = jnp.dot(jnp.concatenate(ps, axis=1), v_sum, preferred_element_type=F32)
        m, num, den = _merge_pair(*ms), both[:, :LANES], both[:, LANES:]
        if cfg > 0:
            rows = pl.ds(n * BLK * dil + r, BLK, stride=dil)
            m_refs[cfg - 1][rows, :], num_refs[cfg - 1][rows, :], den_refs[cfg - 1][rows, :] = m, num, den
            return
        rows = pl.ds(n * BLK, BLK)
        others = [ref[rows, :] for ref in m_refs]
        top = functools.reduce(jnp.maximum, others, m)
        w = jnp.exp2(m - top)
        num, den = w * num, w * den
        for m_c, num_ref, den_ref in zip(others, num_refs, den_refs):
            w = jnp.exp2(m_c - top)
            num, den = num + w * num_ref[rows, :], den + w * den_ref[rows, :]
        o_ref[rows, :] = (num / den).astype(o_ref.dtype)

    assert _DILS[0] == 1
    for cfg in reversed(range(N_CFG)):
        dil = _DILS[cfg]
        for r in range(dil):
            for n in range(DIL_BLOCKS // dil):
                attend(cfg, dil, r, n)
    for cfg, dil in enumerate(_DILS):
        last = DIL_CHUNK // dil - BLK
        prev_k_refs[cfg][...] = k_refs[cfg][:, last:last + BLK, :]
        prev_v_refs[cfg][...] = v_refs[cfg][:, last:last + BLK, :]


def _dilated(qkv, bias_tabs):
    b, _, t, _ = qkv[0].shape
    assert t % DIL_CHUNK == 0
    chunk = lambda d, first_col: pl.BlockSpec((None, d, DIL_CHUNK // d, LANES),
                                              lambda bi, p, c: (bi, 0, c, first_col + p))
    q_specs = [chunk(d, 0) for d in _DILS]
    k_specs = [chunk(d, PAIRS_A) for d in _DILS]
    v_specs = [chunk(d, 2 * PAIRS_A) for d in _DILS]
    carried = [pltpu.VMEM((d, BLK, LANES), BF16) for d in _DILS]
    bias_spec = pl.BlockSpec((N_CFG, 2, 2, BLK, 2 * BLK), lambda bi, p, c: (0, 0, p, 0, 0))
    ones = _head_rows(2 * BLK)
    return pl.pallas_call(
        _dilated_kernel,
        grid=(b, PAIRS_A, t // DIL_CHUNK),
        in_specs=q_specs + k_specs + v_specs
                 + [bias_spec, pl.BlockSpec((4 * BLK, LANES), lambda bi, p, c: (0, 0))],
        out_specs=pl.BlockSpec((None, DIL_CHUNK, LANES), lambda bi, p, c: (bi, c, p)),
        out_shape=jax.ShapeDtypeStruct((b, t, WIDTH_A), BF16),
        scratch_shapes=carried + carried + [pltpu.VMEM((DIL_CHUNK, LANES), F32)] * (3 * (N_CFG - 1)),
        compiler_params=_params(("parallel", "parallel", "arbitrary")),
        name="dilated",
    )(*qkv, *qkv, *qkv, bias_tabs, ones)


def _softplus(z):
    return jnp.maximum(z, 0.0) + jnp.log(1.0 + jnp.exp2(jnp.abs(z) * (-LOG2E)))


def _sb_kernel(q_ref, k_ref, v_ref, tri_ref, heads_ref, o_ref, qs_ref, acc_ref, carry_ref, live_ref):
    step_id = pl.program_id(2)
    bq = SB_BLOCK
    for s in range(SB_SUBS):
        q = q_ref[s * bq:(s + 1) * bq, :]
        qs_ref[s] = jnp.concatenate([q, q], axis=0) * heads_ref[...]
    tri = tri_ref[...]

    def step(qs, j, carry):
        diagonal = carry is None
        ks = pl.multiple_of(j * bq, bq)
        kb = k_ref[pl.ds(ks, bq), :]
        vb = v_ref[pl.ds(ks, bq), :]
        z = lax.dot_general(qs, kb, _NT, preferred_element_type=F32)
        t = _softplus(z)
        if diagonal:
            row = lax.broadcasted_iota(jnp.int32, z.shape, 0) & (bq - 1)
            colv = lax.broadcasted_iota(jnp.int32, z.shape, 1)
            causal = colv < row
            t = jnp.where(causal, t, 0.0)
        after = jnp.dot(t.astype(BF16), tri, preferred_element_type=F32)
        expo = (z - t) + after
        if not diagonal:
            expo = expo + jnp.concatenate([carry] * (bq // LANES), axis=1)
        a = jnp.exp2(expo * LOG2E)
        if diagonal:
            a = jnp.where(causal, a, 0.0)
        pv = jnp.dot(a.astype(BF16), vb, preferred_element_type=F32)
        block_sum = jnp.broadcast_to(after[:, 0:1] - t[:, 0:1], (2 * bq, LANES))
        return pv, block_sum

    def run(s, first_block, count, fresh):
        qs = qs_ref[s]
        acc, carry = (None, None) if fresh else (acc_ref[s], carry_ref[s])
        for u in range(count):
            pv, block_sum = step(qs, first_block - u, carry)
            acc = pv if acc is None else acc + pv
            carry = block_sum if carry is None else carry + block_sum
        acc_ref[s], carry_ref[s] = acc, carry
        return jnp.max(carry)

    @pl.when(step_id == 0)
    def _():
        run(0, 0, 1, True)
        live_ref[0] = SB_DEAD
        for s in range(1, SB_SUBS):
            live_ref[s] = run(s, s, 2, True)

    @pl.when(step_id > 0)
    def _():
        for s in range(SB_SUBS):
            live_ref[s] = run(s, step_id * SB_SUBS + s, 2, True)

    def older(s, c):
        def more(state):
            j, live = state
            return (j >= 1) & (live > SB_DEAD)

        def pair(state):
            j, _ = state
            return j - 2, run(s, j, 2, False)

        j, live = lax.while_loop(more, pair, (step_id * SB_SUBS + s - 2, live_ref[s]))

        @pl.when((j == 0) & (live > SB_DEAD))
        def _():
            run(s, 0, 1, False)

        return c

    @pl.when(functools.reduce(jnp.maximum, [live_ref[s] for s in range(SB_SUBS)]) > SB_DEAD)
    def _():
        lax.fori_loop(0, SB_SUBS, older, 0)

    for s in range(SB_SUBS):
        o_ref[s * bq:(s + 1) * bq, :] = _merge_pair(acc_ref[s, 0:bq, :], acc_ref[s, bq:2 * bq, :]).astype(o_ref.dtype)


def _stick_breaking(qkv):
    b, t, _ = qkv.shape
    bq = SB_BLOCK
    rows = SB_SUBS * bq
    assert t % rows == 0
    r = np.arange(bq)
    tri = jnp.asarray(np.where(r[:, None] > r[None, :], -1.0, 0.0), dtype=BF16)
    heads = _head_rows(bq)
    q_spec = pl.BlockSpec((None, rows, LANES), lambda bi, p, i: (bi, i, p))
    k_spec = pl.BlockSpec((None, t, LANES), lambda bi, p, i: (bi, 0, PAIRS_B + p))
    v_spec = pl.BlockSpec((None, t, LANES), lambda bi, p, i: (bi, 0, 2 * PAIRS_B + p))
    const = lambda shape: pl.BlockSpec(shape, lambda bi, p, i: (0, 0))
    return pl.pallas_call(
        _sb_kernel,
        grid=(b, PAIRS_B, t // rows),
        in_specs=[q_spec, k_spec, v_spec, const((bq, bq)), const((2 * bq, LANES))],
        out_specs=q_spec,
        out_shape=jax.ShapeDtypeStruct((b, t, WIDTH_B), BF16),
        scratch_shapes=[pltpu.VMEM((SB_SUBS, 2 * bq, LANES), BF16), pltpu.VMEM((SB_SUBS, 2 * bq, LANES), F32),
                        pltpu.VMEM((SB_SUBS, 2 * bq, LANES), F32), pltpu.SMEM((SB_SUBS,), F32)],
        compiler_params=_params(("parallel", "parallel", "arbitrary")),
        name="stick_breaking",
    )(qkv, qkv, qkv, tri, heads)


def _out_kernel(*refs, normalize, project):
    ya_ref, yb_ref, gate_ref, h_ref, *refs = refs
    h = h_ref[...]
    if normalize:
        g0_ref, b0_ref, *refs = refs
        h = _layer_norm_rows(h, g0_ref[...], b0_ref[...])
    w_ref, b_ref, g_ref, beta_ref, *refs = refs
    if project:
        w_next_ref, *refs = refs
    hn_ref, *refs = refs
    half_gate = gate_ref[...].astype(F32)
    silu = half_gate + half_gate * jnp.tanh(half_gate)
    ya = (ya_ref[...].astype(F32) * silu[:, :WIDTH_A]).astype(BF16)
    yb = (yb_ref[...].astype(F32) * silu[:, WIDTH_A:]).astype(BF16)
    out = (jnp.dot(ya, w_ref[0:WIDTH_A, :], preferred_element_type=F32)
           + jnp.dot(yb, w_ref[WIDTH_A:MIX_WIDTH, :], preferred_element_type=F32)
           + b_ref[...])
    y = _layer_norm_rows(DEEPNORM_ALPHA * h + out, g_ref[...], beta_ref[...])
    hn_ref[...] = y
    if project:
        _project_rows(y.astype(BF16), w_next_ref, refs)


def _out_block(ya, yb, gate, h, w_out_bf16, layer, b_out, g, beta, batch, ln0=None, w_in_bf16=None):
    n = h.shape[0]
    project = w_in_bf16 is not None
    tile = ROW_TILE if project else OUT_ROW_TILE
    half = pl.BlockSpec((tile, WIDTH_A), lambda i: (i, 0))
    full = pl.BlockSpec((tile, D_MODEL), lambda i: (i, 0))
    vec = pl.BlockSpec((1, D_MODEL), lambda i: (0, 0))
    as_vec = lambda a: a.reshape(1, D_MODEL)
    ln_args = () if ln0 is None else tuple(as_vec(a) for a in ln0)
    in_specs = [half, half, full, full] + [vec] * len(ln_args) + [
        pl.BlockSpec((None, MIX_WIDTH, D_MODEL), lambda i: (layer, 0, 0), pipeline_mode=pl.Buffered(1)),
        vec, vec, vec]
    args = [ya.reshape(n, WIDTH_A), yb.reshape(n, WIDTH_B), gate, h, *ln_args, w_out_bf16,
            as_vec(b_out), as_vec(g), as_vec(beta)]
    out_specs, out_shapes, scratch = [full], [jax.ShapeDtypeStruct((n, D_MODEL), F32)], []
    if project:
        weights, proj_specs, proj_shapes, scratch = _projection_plumbing(n, batch, layer + 1)
        in_specs, args = in_specs + [weights], args + [w_in_bf16]
        out_specs, out_shapes = out_specs + proj_specs, out_shapes + proj_shapes
    outs = pl.pallas_call(
        functools.partial(_out_kernel, normalize=ln0 is not None, project=project),
        grid=(n // tile,),
        in_specs=in_specs,
        out_specs=out_specs,
        out_shape=out_shapes,
        scratch_shapes=scratch,
        compiler_params=_params(("parallel",)),
        name="out_proj_norm",
    )(*args)
    if not project:
        return outs[0], None
    return outs[0], (outs[1:1 + N_CFG], outs[1 + N_CFG], outs[2 + N_CFG])


def kernel(x, ln0_g, ln0_b, w_in, w_out, b_out, ln_g, ln_b, rel_bias):
    b, t, _ = x.shape
    n = b * t
    bias_tabs = _bias_tables(rel_bias)
    w_in, w_out = w_in.astype(BF16), w_out.astype(BF16)
    h = x.reshape(n, D_MODEL)
    ln0 = (ln0_g, ln0_b)
    projections = _project(h, w_in, b, ln0)
    for l in range(DEPTH):
        qkv_a, qkv_b, gate = projections
        yb = _stick_breaking(qkv_b.reshape(b, t, 3 * WIDTH_B))
        ya = _dilated(qkv_a, bias_tabs)
        h, projections = _out_block(ya, yb, gate, h, w_out, l, b_out[l], ln_g[l], ln_b[l], b, ln0=ln0,
                                    w_in_bf16=w_in if l + 1 < DEPTH else None)
        ln0 = None
    return h.reshape(b, t, D_MODEL)
```

```python
import functools
import math

import numpy as np
import jax
import jax.numpy as jnp
from jax import lax
from jax.experimental import pallas as pl
from jax.experimental.pallas import tpu as pltpu

D_MODEL = 1024
DEPTH = 2
HEAD_DIM = 64
N_HEADS_A = 8
N_HEADS_B = 8
WIDTH_A = N_HEADS_A * HEAD_DIM
WIDTH_B = N_HEADS_B * HEAD_DIM
MIX_WIDTH = WIDTH_A + WIDTH_B
IN_COLS = 3 * WIDTH_A + 3 * WIDTH_B + MIX_WIDTH
DILATED_CONFIGS = ((128, 1), (512, 4), (2048, 16))
N_CFG = len(DILATED_CONFIGS)
BLK = 128
N_BUCKETS = 32
MAX_DISTANCE = 2048
LN_EPS = 1e-5
DEEPNORM_ALPHA = (2.0 * DEPTH) ** 0.25
QK_SCALE = 1.0 / math.sqrt(HEAD_DIM)
LOG2E = math.log2(math.e)

LANES = 128
BF16_ROWS = 16
PAIRS_A = WIDTH_A // LANES
PAIRS_B = WIDTH_B // LANES
NEG_BIG = -1e30

ROW_TILE = 512
OUT_ROW_TILE = 1024
OUT_BUFFERS = 3
SB_BLOCK = 256
SB_SUBS = 16
SB_DEAD = -110.0
DIL_CHUNK = 2 * BLK * max(d for _, d in DILATED_CONFIGS)
DIL_BLOCKS = DIL_CHUNK // BLK
VMEM_LIMIT = 48 * 1024 * 1024

F32 = jnp.float32
BF16 = jnp.bfloat16


def _params(semantics):
    return pltpu.CompilerParams(dimension_semantics=semantics, vmem_limit_bytes=VMEM_LIMIT)


def _layer_norm_rows(x, g, b):
    mu = jnp.mean(x, axis=-1, keepdims=True)
    xc = x - mu
    var = jnp.mean(xc * xc, axis=-1, keepdims=True)
    return xc * lax.rsqrt(var + LN_EPS) * g + b


_PROJ_A = ((0, QK_SCALE * LOG2E), (WIDTH_A, 1.0), (2 * WIDTH_A, 1.0))
_PROJ_B = ((3 * WIDTH_A, QK_SCALE), (3 * WIDTH_A + WIDTH_B, 1.0), (3 * WIDTH_A + 2 * WIDTH_B, 1.0))
_GATE_COL = 3 * WIDTH_A + 3 * WIDTH_B
_DILS = tuple(d for _, d in DILATED_CONFIGS)


def _proj_kernel(x_ref, g_ref, b_ref, w_ref, *refs):
    _project_rows(_layer_norm_rows(x_ref[...], g_ref[...], b_ref[...]).astype(BF16), w_ref, refs)


def _project_rows(h, w_ref, refs):
    a_refs, (b_ref, gate_ref), fold_refs = refs[:N_CFG], refs[N_CFG:N_CFG + 2], refs[N_CFG + 2:]
    slabs = range(WIDTH_A // LANES)
    for t, (c0, scale) in enumerate(_PROJ_A):
        cols = slice(t * WIDTH_A, (t + 1) * WIDTH_A)
        acc = jnp.dot(h, w_ref[:, c0:c0 + WIDTH_A], preferred_element_type=F32)
        if scale != 1.0:
            acc = acc * scale
        a_refs[0][0, :, cols] = acc.astype(BF16)
        src_ref, src_dil = fold_refs[2 * t], 1
        for j in slabs:
            src_ref[j] = acc[:, j * LANES:(j + 1) * LANES]
        for cfg in range(1, N_CFG):
            dil, dst_ref = _DILS[cfg], fold_refs[2 * t + cfg % 2]
            step, rows_out = dil // src_dil, ROW_TILE // dil
            for r_src in range(src_dil):
                for a in range(step):
                    r = r_src + src_dil * a
                    rows = pl.ds(r_src * (ROW_TILE // src_dil) + a, rows_out, stride=step)
                    parts = [src_ref[j, rows, :] for j in slabs]
                    a_refs[cfg][r, :, cols] = jnp.concatenate(parts, axis=1).astype(BF16)
                    if cfg + 1 < N_CFG:
                        for j in slabs:
                            dst_ref[j, r * rows_out:(r + 1) * rows_out, :] = parts[j]
            src_ref, src_dil = dst_ref, dil
    for t, (c0, scale) in enumerate(_PROJ_B):
        acc = jnp.dot(h, w_ref[:, c0:c0 + WIDTH_B], preferred_element_type=F32)
        if scale != 1.0:
            acc = acc * scale
        b_ref[:, t * WIDTH_B:(t + 1) * WIDTH_B] = acc.astype(BF16)
    gate = jnp.dot(h, w_ref[:, _GATE_COL:_GATE_COL + MIX_WIDTH], preferred_element_type=F32)
    gate_ref[...] = (0.5 * gate).astype(BF16)


def _projection_plumbing(n, batch, layer):
    t = n // batch
    tiles = t // ROW_TILE
    assert t % ROW_TILE == 0 and all(ROW_TILE % (BF16_ROWS * d) == 0 for d in _DILS)
    weights = pl.BlockSpec((None, D_MODEL, IN_COLS), lambda i: (layer, 0, 0),
                           pipeline_mode=pl.Buffered(1))
    a_specs = [pl.BlockSpec((None, d, ROW_TILE // d, 3 * WIDTH_A), lambda i: (i // tiles, 0, i % tiles, 0))
               for d in _DILS]
    a_shapes = [jax.ShapeDtypeStruct((batch, d, t // d, 3 * WIDTH_A), BF16) for d in _DILS]
    b_widths = (3 * WIDTH_B, MIX_WIDTH)
    out_specs = a_specs + [pl.BlockSpec((ROW_TILE, w), lambda i: (i, 0)) for w in b_widths]
    out_shapes = a_shapes + [jax.ShapeDtypeStruct((n, w), BF16) for w in b_widths]
    scratch = [pltpu.VMEM((WIDTH_A // LANES, ROW_TILE, LANES), F32)] * (2 * len(_PROJ_A))
    return weights, out_specs, out_shapes, scratch


def _project(x2, w_bf16, batch, ln0):
    n = x2.shape[0]
    row = pl.BlockSpec((ROW_TILE, D_MODEL), lambda i: (i, 0))
    vec = pl.BlockSpec((1, D_MODEL), lambda i: (0, 0))
    weights, out_specs, out_shapes, scratch = _projection_plumbing(n, batch, 0)
    outs = pl.pallas_call(
        _proj_kernel,
        grid=(n // ROW_TILE,),
        in_specs=[row, vec, vec, weights],
        out_specs=out_specs,
        out_shape=out_shapes,
        scratch_shapes=scratch,
        compiler_params=_params(("parallel",)),
        name="in_proj",
    )(x2, *(a.reshape(1, D_MODEL) for a in ln0), w_bf16)
    return outs[:N_CFG], outs[N_CFG], outs[N_CFG + 1]


def _head_rows(rows_per_head):
    head_of_lane = np.arange(LANES) // HEAD_DIM
    head_of_row = np.arange(2 * rows_per_head) // rows_per_head
    return jnp.asarray(head_of_row[:, None] == head_of_lane[None, :], dtype=BF16)


def _merge_pair(x0, x1):
    lane = lax.broadcasted_iota(jnp.int32, x0.shape, 1)
    return jnp.where(lane < HEAD_DIM, x0, x1)


_NT = (((1,), (1,)), ((), ()))


def _t5_bucket(dist):
    max_exact = N_BUCKETS // 2
    n = np.maximum(dist, 1).astype(np.float32)
    large = max_exact + (np.log(n / max_exact) / math.log(MAX_DISTANCE / max_exact)
                         * (N_BUCKETS - max_exact)).astype(np.int32)
    large = np.minimum(large, N_BUCKETS - 1)
    return np.where(dist < max_exact, dist, large).astype(np.int32)


def _bucket_segments(window, dil):
    buckets = _t5_bucket(np.arange(window // dil + 1) * dil)
    return [(d, int(bk)) for d, bk in enumerate(buckets) if d == 0 or bk != buckets[d - 1]]


def _bias_kernel(rel_ref, out_ref):
    i = lax.broadcasted_iota(jnp.int32, (BLK, 2 * BLK), 0)
    j = lax.broadcasted_iota(jnp.int32, (BLK, 2 * BLK), 1)
    delta = BLK + i - j
    for cfg, (window, dil) in enumerate(DILATED_CONFIGS):
        segments = _bucket_segments(window, dil)
        for variant in range(2):
            band = (delta >= 0) & (delta <= window // dil) & (j >= variant * BLK)
            steps = [delta >= start for start, _ in segments[1:]]
            for h in range(N_HEADS_A):
                val = jnp.full((BLK, 2 * BLK), rel_ref[segments[0][1], h] * LOG2E, F32)
                for step, (_, bucket) in zip(steps, segments[1:]):
                    val = jnp.where(step, rel_ref[bucket, h] * LOG2E, val)
                out_ref[cfg, variant, h] = jnp.where(band, val, NEG_BIG)


def _bias_tables(rel_bias):
    shape = (N_CFG, 2, N_HEADS_A, BLK, 2 * BLK)
    return pl.pallas_call(
        _bias_kernel,
        in_specs=[pl.BlockSpec(memory_space=pltpu.SMEM)],
        out_shape=jax.ShapeDtypeStruct(shape, F32),
        compiler_params=pltpu.CompilerParams(vmem_limit_bytes=VMEM_LIMIT),
        name="bias_tables",
    )(rel_bias.astype(F32))


def _dilated_kernel(*refs):
    q_refs, k_refs, v_refs = refs[:N_CFG], refs[N_CFG:2 * N_CFG], refs[2 * N_CFG:3 * N_CFG]
    bias_ref, ones_ref, o_ref, *scratch_refs = refs[3 * N_CFG:]
    prev_k_refs, prev_v_refs, state_refs = scratch_refs[:N_CFG], scratch_refs[N_CFG:2 * N_CFG], scratch_refs[2 * N_CFG:]
    c = pl.program_id(2)
    n_state = N_CFG - 1
    m_refs, den_refs, num_refs = state_refs[:n_state], state_refs[n_state:2 * n_state], state_refs[2 * n_state:]

    @pl.when(c == 0)
    def _():
        for cfg in range(N_CFG):
            prev_k_refs[cfg][...] = k_refs[cfg][:, 0:BLK, :]
            prev_v_refs[cfg][...] = v_refs[cfg][:, 0:BLK, :]

    def key_rows(ref, prev_ref, r, n):
        if n == 0:
            return jnp.concatenate([prev_ref[r], ref[r, 0:BLK, :]], axis=0)
        return ref[r, (n - 1) * BLK:(n + 1) * BLK, :]

    def attend(cfg, dil, r, n):
        first = jnp.where(c == 0, 1, 0) if n == 0 else 0
        ones = ones_ref[...]
        q = q_refs[cfg][r, n * BLK:(n + 1) * BLK, :]
        q_pair = (q * ones[0:BLK], q * ones[2 * BLK:3 * BLK])
        kb = key_rows(k_refs[cfg], prev_k_refs[cfg], r, n)
        vb = key_rows(v_refs[cfg], prev_v_refs[cfg], r, n)
        v_sum = jnp.concatenate([jnp.concatenate([vb, vb], axis=0) * ones, ones], axis=1)
        ms, ps = [], []
        for h in range(2):
            s = lax.dot_general(q_pair[h], kb, _NT, preferred_element_type=F32) + bias_ref[cfg, first, h]
            m = jnp.max(s, axis=-1, keepdims=True)
            ps.append(jnp.exp2(s - m).astype(BF16))
            ms.append(jnp.broadcast_to(m, (BLK, LANES)))
        both = jnp.dot(jnp.concatenate(ps, axis=1), v_sum, preferred_element_type=F32)
        m, num, den = _merge_pair(*ms), both[:, :LANES], both[:, LANES:]
        if cfg > 0:
            rows = pl.ds(n * BLK * dil + r, BLK, stride=dil)
            m_refs[cfg - 1][rows, :], num_refs[cfg - 1][rows, :], den_refs[cfg - 1][rows, :] = m, num, den
            return
        rows = pl.ds(n * BLK, BLK)
        others = [ref[rows, :] for ref in m_refs]
        top = functools.reduce(jnp.maximum, others, m)
        w = jnp.exp2(m - top)
        num, den = w * num, w * den
        for m_c, num_ref, den_ref in zip(others, num_refs, den_refs):
            w = jnp.exp2(m_c - top)
            num, den = num + w * num_ref[rows, :], den + w * den_ref[rows, :]
        o_ref[rows, :] = (num / den).astype(o_ref.dtype)

    assert _DILS[0] == 1
    for cfg in reversed(range(N_CFG)):
        dil = _DILS[cfg]
        for r in range(dil):
            for n in range(DIL_BLOCKS // dil):
                attend(cfg, dil, r, n)
    for cfg, dil in enumerate(_DILS):
        last = DIL_CHUNK // dil - BLK
        prev_k_refs[cfg][...] = k_refs[cfg][:, last:last + BLK, :]
        prev_v_refs[cfg][...] = v_refs[cfg][:, last:last + BLK, :]


def _dilated(qkv, bias_tabs):
    b, _, t, _ = qkv[0].shape
    assert t % DIL_CHUNK == 0
    chunk = lambda d, first_col: pl.BlockSpec((None, d, DIL_CHUNK // d, LANES),
                                              lambda bi, p, c: (bi, 0, c, first_col + p))
    q_specs = [chunk(d, 0) for d in _DILS]
    k_specs = [chunk(d, PAIRS_A) for d in _DILS]
    v_specs = [chunk(d, 2 * PAIRS_A) for d in _DILS]
    carried = [pltpu.VMEM((d, BLK, LANES), BF16) for d in _DILS]
    bias_spec = pl.BlockSpec((N_CFG, 2, 2, BLK, 2 * BLK), lambda bi, p, c: (0, 0, p, 0, 0))
    ones = _head_rows(2 * BLK)
    return pl.pallas_call(
        _dilated_kernel,
        grid=(b, PAIRS_A, t // DIL_CHUNK),
        in_specs=q_specs + k_specs + v_specs
                 + [bias_spec, pl.BlockSpec((4 * BLK, LANES), lambda bi, p, c: (0, 0))],
        out_specs=pl.BlockSpec((None, DIL_CHUNK, LANES), lambda bi, p, c: (bi, c, p)),
        out_shape=jax.ShapeDtypeStruct((b, t, WIDTH_A), BF16),
        scratch_shapes=carried + carried + [pltpu.VMEM((DIL_CHUNK, LANES), F32)] * (3 * (N_CFG - 1)),
        compiler_params=_params(("parallel", "parallel", "arbitrary")),
        name="dilated",
    )(*qkv, *qkv, *qkv, bias_tabs, ones)


def _softplus(z):
    return jnp.maximum(z, 0.0) + jnp.log(1.0 + jnp.exp2(jnp.abs(z) * (-LOG2E)))


def _sb_kernel(q_ref, k_ref, v_ref, tri_ref, heads_ref, o_ref, qs_ref, acc_ref, carry_ref, live_ref):
    step_id = pl.program_id(2)
    bq = SB_BLOCK
    for s in range(SB_SUBS):
        q = q_ref[s * bq:(s + 1) * bq, :]
        qs_ref[s] = jnp.concatenate([q, q], axis=0) * heads_ref[...]
    tri = tri_ref[...]

    def step(qs, j, carry):
        diagonal = carry is None
        ks = pl.multiple_of(j * bq, bq)
        kb = k_ref[pl.ds(ks, bq), :]
        vb = v_ref[pl.ds(ks, bq), :]
        z = lax.dot_general(qs, kb, _NT, preferred_element_type=F32)
        t = _softplus(z)
        if diagonal:
            row = lax.broadcasted_iota(jnp.int32, z.shape, 0) & (bq - 1)
            colv = lax.broadcasted_iota(jnp.int32, z.shape, 1)
            causal = colv < row
            t = jnp.where(causal, t, 0.0)
        after = jnp.dot(t.astype(BF16), tri, preferred_element_type=F32)
        expo = (z - t) + after
        if not diagonal:
            expo = expo + jnp.concatenate([carry] * (bq // LANES), axis=1)
        a = jnp.exp2(expo * LOG2E)
        if diagonal:
            a = jnp.where(causal, a, 0.0)
        pv = jnp.dot(a.astype(BF16), vb, preferred_element_type=F32)
        block_sum = jnp.broadcast_to(after[:, 0:1] - t[:, 0:1], (2 * bq, LANES))
        return pv, block_sum

    def run(s, first_block, count, fresh):
        qs = qs_ref[s]
        acc, carry = (None, None) if fresh else (acc_ref[s], carry_ref[s])
        for u in range(count):
            pv, block_sum = step(qs, first_block - u, carry)
            acc = pv if acc is None else acc + pv
            carry = block_sum if carry is None else carry + block_sum
        acc_ref[s], carry_ref[s] = acc, carry
        return jnp.max(carry)

    @pl.when(step_id == 0)
    def _():
        run(0, 0, 1, True)
        live_ref[0] = SB_DEAD
        for s in range(1, SB_SUBS):
            live_ref[s] = run(s, s, 2, True)

    @pl.when(step_id > 0)
    def _():
        for s in range(SB_SUBS):
            live_ref[s] = run(s, step_id * SB_SUBS + s, 2, True)

    def older(s, c):
        def more(state):
            j, live = state
            return (j >= 1) & (live > SB_DEAD)

        def pair(state):
            j, _ = state
            return j - 2, run(s, j, 2, False)

        j, live = lax.while_loop(more, pair, (step_id * SB_SUBS + s - 2, live_ref[s]))

        @pl.when((j == 0) & (live > SB_DEAD))
        def _():
            run(s, 0, 1, False)

        return c

    @pl.when(functools.reduce(jnp.maximum, [live_ref[s] for s in range(SB_SUBS)]) > SB_DEAD)
    def _():
        lax.fori_loop(0, SB_SUBS, older, 0)

    for s in range(SB_SUBS):
        o_ref[s * bq:(s + 1) * bq, :] = _merge_pair(acc_ref[s, 0:bq, :], acc_ref[s, bq:2 * bq, :]).astype(o_ref.dtype)


def _stick_breaking(qkv):
    b, t, _ = qkv.shape
    bq = SB_BLOCK
    rows = SB_SUBS * bq
    assert t % rows == 0
    r = np.arange(bq)
    tri = jnp.asarray(np.where(r[:, None] > r[None, :], -1.0, 0.0), dtype=BF16)
    heads = _head_rows(bq)
    q_spec = pl.BlockSpec((None, rows, LANES), lambda bi, p, i: (bi, i, p))
    k_spec = pl.BlockSpec((None, t, LANES), lambda bi, p, i: (bi, 0, PAIRS_B + p))
    v_spec = pl.BlockSpec((None, t, LANES), lambda bi, p, i: (bi, 0, 2 * PAIRS_B + p))
    const = lambda shape: pl.BlockSpec(shape, lambda bi, p, i: (0, 0))
    return pl.pallas_call(
        _sb_kernel,
        grid=(b, PAIRS_B, t // rows),
        in_specs=[q_spec, k_spec, v_spec, const((bq, bq)), const((2 * bq, LANES))],
        out_specs=q_spec,
        out_shape=jax.ShapeDtypeStruct((b, t, WIDTH_B), BF16),
        scratch_shapes=[pltpu.VMEM((SB_SUBS, 2 * bq, LANES), BF16), pltpu.VMEM((SB_SUBS, 2 * bq, LANES), F32),
                        pltpu.VMEM((SB_SUBS, 2 * bq, LANES), F32), pltpu.SMEM((SB_SUBS,), F32)],
        compiler_params=_params(("parallel", "parallel", "arbitrary")),
        name="stick_breaking",
    )(qkv, qkv, qkv, tri, heads)


def _out_kernel(*refs, normalize, project):
    ya_ref, yb_ref, gate_ref, h_ref, *refs = refs
    h = h_ref[...]
    if normalize:
        g0_ref, b0_ref, *refs = refs
        h = _layer_norm_rows(h, g0_ref[...], b0_ref[...])
    w_ref, b_ref, g_ref, beta_ref, *refs = refs
    if project:
        w_next_ref, *refs = refs
    hn_ref, *refs = refs
    half_gate = gate_ref[...].astype(F32)
    silu = half_gate + half_gate * jnp.tanh(half_gate)
    ya = (ya_ref[...].astype(F32) * silu[:, :WIDTH_A]).astype(BF16)
    yb = (yb_ref[...].astype(F32) * silu[:, WIDTH_A:]).astype(BF16)
    out = (jnp.dot(ya, w_ref[0:WIDTH_A, :], preferred_element_type=F32)
           + jnp.dot(yb, w_ref[WIDTH_A:MIX_WIDTH, :], preferred_element_type=F32)
           + b_ref[...])
    y = _layer_norm_rows(DEEPNORM_ALPHA * h + out, g_ref[...], beta_ref[...])
    hn_ref[...] = y
    if project:
        _project_rows(y.astype(BF16), w_next_ref, refs)


def _out_block(ya, yb, gate, h, w_out_bf16, layer, b_out, g, beta, batch, ln0=None, w_in_bf16=None):
    n = h.shape[0]
    project = w_in_bf16 is not None
    tile = ROW_TILE if project else OUT_ROW_TILE
    half = pl.BlockSpec((tile, WIDTH_A), lambda i: (i, 0))
    full = pl.BlockSpec((tile, D_MODEL), lambda i: (i, 0))
    vec = pl.BlockSpec((1, D_MODEL), lambda i: (0, 0))
    as_vec = lambda a: a.reshape(1, D_MODEL)
    ln_args = () if ln0 is None else tuple(as_vec(a) for a in ln0)
    in_specs = [half, half, full, full] + [vec] * len(ln_args) + [
        pl.BlockSpec((None, MIX_WIDTH, D_MODEL), lambda i: (layer, 0, 0), pipeline_mode=pl.Buffered(1)),
        vec, vec, vec]
    args = [ya.reshape(n, WIDTH_A), yb.reshape(n, WIDTH_B), gate, h, *ln_args, w_out_bf16,
            as_vec(b_out), as_vec(g), as_vec(beta)]
    out_specs, out_shapes, scratch = [full], [jax.ShapeDtypeStruct((n, D_MODEL), F32)], []
    if not project and ln0 is None:
        deep = lambda spec: pl.BlockSpec(spec.block_shape, spec.index_map, pipeline_mode=pl.Buffered(OUT_BUFFERS))
        whole = pl.BlockSpec((MIX_WIDTH, D_MODEL), lambda i: (0, 0))

        def stream(*refs):
            *ins, w_ref, b_ref, g_ref, beta_ref, o_ref = refs
            pltpu.emit_pipeline(
                functools.partial(_out_kernel, normalize=False, project=False),
                grid=(n // tile,),
                in_specs=[deep(half), deep(half), deep(full), deep(full), whole, vec, vec, vec],
                out_specs=[full],
            )(*ins, w_ref.at[layer], b_ref, g_ref, beta_ref, o_ref)

        hbm = pl.BlockSpec(memory_space=pl.ANY)
        out = pl.pallas_call(
            stream,
            in_specs=[hbm] * len(args),
            out_specs=hbm,
            out_shape=out_shapes[0],
            compiler_params=pltpu.CompilerParams(vmem_limit_bytes=VMEM_LIMIT),
            name="out_proj_norm",
        )(*args)
        return out, None
    if project:
        weights, proj_specs, proj_shapes, scratch = _projection_plumbing(n, batch, layer + 1)
        in_specs, args = in_specs + [weights], args + [w_in_bf16]
        out_specs, out_shapes = out_specs + proj_specs, out_shapes + proj_shapes
    outs = pl.pallas_call(
        functools.partial(_out_kernel, normalize=ln0 is not None, project=project),
        grid=(n // tile,),
        in_specs=in_specs,
        out_specs=out_specs,
        out_shape=out_shapes,
        scratch_shapes=scratch,
        compiler_params=_params(("parallel",)),
        name="out_proj_norm",
    )(*args)
    if not project:
        return outs[0], None
    return outs[0], (outs[1:1 + N_CFG], outs[1 + N_CFG], outs[2 + N_CFG])


def kernel(x, ln0_g, ln0_b, w_in, w_out, b_out, ln_g, ln_b, rel_bias):
    b, t, _ = x.shape
    n = b * t
    bias_tabs = _bias_tables(rel_bias)
    w_in, w_out = w_in.astype(BF16), w_out.astype(BF16)
    h = x.reshape(n, D_MODEL)
    ln0 = (ln0_g, ln0_b)
    projections = _project(h, w_in, b, ln0)
    for l in range(DEPTH):
        qkv_a, qkv_b, gate = projections
        yb = _stick_breaking(qkv_b.reshape(b, t, 3 * WIDTH_B))
        ya = _dilated(qkv_a, bias_tabs)
        h, projections = _out_block(ya, yb, gate, h, w_out, l, b_out[l], ln_g[l], ln_b[l], b, ln0=ln0,
                                    w_in_bf16=w_in if l + 1 < DEPTH else None)
        ln0 = None
    return h.reshape(b, t, D_MODEL)
```

```python
import functools
import math

import numpy as np
import jax
import jax.numpy as jnp
from jax import lax
from jax.experimental import pallas as pl
from jax.experimental.pallas import tpu as pltpu

D_MODEL = 1024
DEPTH = 2
HEAD_DIM = 64
N_HEADS_A = 8
N_HEADS_B = 8
WIDTH_A = N_HEADS_A * HEAD_DIM
WIDTH_B = N_HEADS_B * HEAD_DIM
MIX_WIDTH = WIDTH_A + WIDTH_B
IN_COLS = 3 * WIDTH_A + 3 * WIDTH_B + MIX_WIDTH
DILATED_CONFIGS = ((128, 1), (512, 4), (2048, 16))
N_CFG = len(DILATED_CONFIGS)
BLK = 128
N_BUCKETS = 32
MAX_DISTANCE = 2048
LN_EPS = 1e-5
DEEPNORM_ALPHA = (2.0 * DEPTH) ** 0.25
QK_SCALE = 1.0 / math.sqrt(HEAD_DIM)
LOG2E = math.log2(math.e)

LANES = 128
BF16_ROWS = 16
PAIRS_A = WIDTH_A // LANES
PAIRS_B = WIDTH_B // LANES
NEG_BIG = -1e30

ROW_TILE = 512
OUT_ROW_TILE = 1024
OUT_BUFFERS = 3
SB_BLOCK = 256
SB_SUBS = 16
SB_DEAD = -110.0
DIL_CHUNK = 2 * BLK * max(d for _, d in DILATED_CONFIGS)
DIL_BLOCKS = DIL_CHUNK // BLK
VMEM_LIMIT = 48 * 1024 * 1024

F32 = jnp.float32
BF16 = jnp.bfloat16


def _params(semantics):
    return pltpu.CompilerParams(dimension_semantics=semantics, vmem_limit_bytes=VMEM_LIMIT)


def _layer_norm_rows(x, g, b):
    mu = jnp.mean(x, axis=-1, keepdims=True)
    xc = x - mu
    var = jnp.mean(xc * xc, axis=-1, keepdims=True)
    return xc * lax.rsqrt(var + LN_EPS) * g + b


_PROJ_A = ((0, QK_SCALE * LOG2E), (WIDTH_A, 1.0), (2 * WIDTH_A, 1.0))
_PROJ_B = ((3 * WIDTH_A, QK_SCALE), (3 * WIDTH_A + WIDTH_B, 1.0), (3 * WIDTH_A + 2 * WIDTH_B, 1.0))
_GATE_COL = 3 * WIDTH_A + 3 * WIDTH_B
_DILS = tuple(d for _, d in DILATED_CONFIGS)


def _proj_kernel(x_ref, g_ref, b_ref, w_ref, *refs):
    _project_rows(_layer_norm_rows(x_ref[...], g_ref[...], b_ref[...]).astype(BF16), w_ref, refs)


def _project_rows(h, w_ref, refs):
    a_refs, (b_ref, gate_ref), fold_refs = refs[:N_CFG], refs[N_CFG:N_CFG + 2], refs[N_CFG + 2:]
    slabs = range(WIDTH_A // LANES)
    for t, (c0, scale) in enumerate(_PROJ_A):
        cols = slice(t * WIDTH_A, (t + 1) * WIDTH_A)
        acc = jnp.dot(h, w_ref[:, c0:c0 + WIDTH_A], preferred_element_type=F32)
        if scale != 1.0:
            acc = acc * scale
        a_refs[0][0, :, cols] = acc.astype(BF16)
        src_ref, src_dil = fold_refs[2 * t], 1
        for j in slabs:
            src_ref[j] = acc[:, j * LANES:(j + 1) * LANES]
        for cfg in range(1, N_CFG):
            dil, dst_ref = _DILS[cfg], fold_refs[2 * t + cfg % 2]
            step, rows_out = dil // src_dil, ROW_TILE // dil
            for r_src in range(src_dil):
                for a in range(step):
                    r = r_src + src_dil * a
                    rows = pl.ds(r_src * (ROW_TILE // src_dil) + a, rows_out, stride=step)
                    parts = [src_ref[j, rows, :] for j in slabs]
                    a_refs[cfg][r, :, cols] = jnp.concatenate(parts, axis=1).astype(BF16)
                    if cfg + 1 < N_CFG:
                        for j in slabs:
                            dst_ref[j, r * rows_out:(r + 1) * rows_out, :] = parts[j]
            src_ref, src_dil = dst_ref, dil
    for t, (c0, scale) in enumerate(_PROJ_B):
        acc = jnp.dot(h, w_ref[:, c0:c0 + WIDTH_B], preferred_element_type=F32)
        if scale != 1.0:
            acc = acc * scale
        b_ref[:, t * WIDTH_B:(t + 1) * WIDTH_B] = acc.astype(BF16)
    gate = jnp.dot(h, w_ref[:, _GATE_COL:_GATE_COL + MIX_WIDTH], preferred_element_type=F32)
    gate_ref[...] = (0.5 * gate).astype(BF16)


def _projection_plumbing(n, batch, layer):
    t = n // batch
    tiles = t // ROW_TILE
    assert t % ROW_TILE == 0 and all(ROW_TILE % (BF16_ROWS * d) == 0 for d in _DILS)
    weights = pl.BlockSpec((None, D_MODEL, IN_COLS), lambda i: (layer, 0, 0),
                           pipeline_mode=pl.Buffered(1))
    a_specs = [pl.BlockSpec((None, d, ROW_TILE // d, 3 * WIDTH_A), lambda i: (i // tiles, 0, i % tiles, 0))
               for d in _DILS]
    a_shapes = [jax.ShapeDtypeStruct((batch, d, t // d, 3 * WIDTH_A), BF16) for d in _DILS]
    b_widths = (3 * WIDTH_B, MIX_WIDTH)
    out_specs = a_specs + [pl.BlockSpec((ROW_TILE, w), lambda i: (i, 0)) for w in b_widths]
    out_shapes = a_shapes + [jax.ShapeDtypeStruct((n, w), BF16) for w in b_widths]
    scratch = [pltpu.VMEM((WIDTH_A // LANES, ROW_TILE, LANES), F32)] * (2 * len(_PROJ_A))
    return weights, out_specs, out_shapes, scratch


def _project(x2, w_bf16, batch, ln0):
    n = x2.shape[0]
    vec = pl.BlockSpec((1, D_MODEL), lambda i: (0, 0))
    _, out_specs, out_shapes, scratch = _projection_plumbing(n, batch, 0)
    deep_row = pl.BlockSpec((ROW_TILE, D_MODEL), lambda i: (i, 0), pipeline_mode=pl.Buffered(OUT_BUFFERS))

    def stream(x_hbm, g_hbm, b_hbm, w_hbm, *refs):
        out_hbm, (w_ref, *fold_refs) = refs[:len(out_specs)], refs[len(out_specs):]
        pltpu.sync_copy(w_hbm.at[0], w_ref)

        def body(x_ref, g_ref, b_ref, *out_refs):
            _proj_kernel(x_ref, g_ref, b_ref, w_ref, *out_refs, *fold_refs)

        pltpu.emit_pipeline(body, grid=(n // ROW_TILE,), in_specs=[deep_row, vec, vec],
                            out_specs=out_specs)(x_hbm, g_hbm, b_hbm, *out_hbm)

    hbm = pl.BlockSpec(memory_space=pl.ANY)
    outs = pl.pallas_call(
        stream,
        in_specs=[hbm] * 4,
        out_specs=[hbm] * len(out_specs),
        out_shape=out_shapes,
        scratch_shapes=[pltpu.VMEM((D_MODEL, IN_COLS), BF16)] + scratch,
        compiler_params=pltpu.CompilerParams(vmem_limit_bytes=VMEM_LIMIT),
        name="in_proj",
    )(x2, *(a.reshape(1, D_MODEL) for a in ln0), w_bf16)
    return outs[:N_CFG], outs[N_CFG], outs[N_CFG + 1]


def _head_rows(rows_per_head):
    head_of_lane = np.arange(LANES) // HEAD_DIM
    head_of_row = np.arange(2 * rows_per_head) // rows_per_head
    return jnp.asarray(head_of_row[:, None] == head_of_lane[None, :], dtype=BF16)


def _merge_pair(x0, x1):
    lane = lax.broadcasted_iota(jnp.int32, x0.shape, 1)
    return jnp.where(lane < HEAD_DIM, x0, x1)


_NT = (((1,), (1,)), ((), ()))


def _t5_bucket(dist):
    max_exact = N_BUCKETS // 2
    n = np.maximum(dist, 1).astype(np.float32)
    large = max_exact + (np.log(n / max_exact) / math.log(MAX_DISTANCE / max_exact)
                         * (N_BUCKETS - max_exact)).astype(np.int32)
    large = np.minimum(large, N_BUCKETS - 1)
    return np.where(dist < max_exact, dist, large).astype(np.int32)


def _bucket_segments(window, dil):
    buckets = _t5_bucket(np.arange(window // dil + 1) * dil)
    return [(d, int(bk)) for d, bk in enumerate(buckets) if d == 0 or bk != buckets[d - 1]]


def _bias_kernel(rel_ref, out_ref):
    i = lax.broadcasted_iota(jnp.int32, (BLK, 2 * BLK), 0)
    j = lax.broadcasted_iota(jnp.int32, (BLK, 2 * BLK), 1)
    delta = BLK + i - j
    for cfg, (window, dil) in enumerate(DILATED_CONFIGS):
        segments = _bucket_segments(window, dil)
        for variant in range(2):
            band = (delta >= 0) & (delta <= window // dil) & (j >= variant * BLK)
            steps = [delta >= start for start, _ in segments[1:]]
            for h in range(N_HEADS_A):
                val = jnp.full((BLK, 2 * BLK), rel_ref[segments[0][1], h] * LOG2E, F32)
                for step, (_, bucket) in zip(steps, segments[1:]):
                    val = jnp.where(step, rel_ref[bucket, h] * LOG2E, val)
                out_ref[cfg, variant, h] = jnp.where(band, val, NEG_BIG)


def _bias_tables(rel_bias):
    shape = (N_CFG, 2, N_HEADS_A, BLK, 2 * BLK)
    return pl.pallas_call(
        _bias_kernel,
        in_specs=[pl.BlockSpec(memory_space=pltpu.SMEM)],
        out_shape=jax.ShapeDtypeStruct(shape, F32),
        compiler_params=pltpu.CompilerParams(vmem_limit_bytes=VMEM_LIMIT),
        name="bias_tables",
    )(rel_bias.astype(F32))


def _dilated_kernel(*refs):
    q_refs, k_refs, v_refs = refs[:N_CFG], refs[N_CFG:2 * N_CFG], refs[2 * N_CFG:3 * N_CFG]
    bias_ref, ones_ref, o_ref, *scratch_refs = refs[3 * N_CFG:]
    prev_k_refs, prev_v_refs, state_refs = scratch_refs[:N_CFG], scratch_refs[N_CFG:2 * N_CFG], scratch_refs[2 * N_CFG:]
    c = pl.program_id(2)
    n_state = N_CFG - 1
    m_refs, den_refs, num_refs = state_refs[:n_state], state_refs[n_state:2 * n_state], state_refs[2 * n_state:]

    @pl.when(c == 0)
    def _():
        for cfg in range(N_CFG):
            prev_k_refs[cfg][...] = k_refs[cfg][:, 0:BLK, :]
            prev_v_refs[cfg][...] = v_refs[cfg][:, 0:BLK, :]

    def key_rows(ref, prev_ref, r, n):
        if n == 0:
            return jnp.concatenate([prev_ref[r], ref[r, 0:BLK, :]], axis=0)
        return ref[r, (n - 1) * BLK:(n + 1) * BLK, :]

    def attend(cfg, dil, r, n):
        first = jnp.where(c == 0, 1, 0) if n == 0 else 0
        ones = ones_ref[...]
        q = q_refs[cfg][r, n * BLK:(n + 1) * BLK, :]
        q_pair = (q * ones[0:BLK], q * ones[2 * BLK:3 * BLK])
        kb = key_rows(k_refs[cfg], prev_k_refs[cfg], r, n)
        vb = key_rows(v_refs[cfg], prev_v_refs[cfg], r, n)
        v_sum = jnp.concatenate([jnp.concatenate([vb, vb], axis=0) * ones, ones], axis=1)
        ms, ps = [], []
        for h in range(2):
            s = lax.dot_general(q_pair[h], kb, _NT, preferred_element_type=F32) + bias_ref[cfg, first, h]
            m = jnp.max(s, axis=-1, keepdims=True)
            ps.append(jnp.exp2(s - m).astype(BF16))
            ms.append(jnp.broadcast_to(m, (BLK, LANES)))
        both = jnp.dot(jnp.concatenate(ps, axis=1), v_sum, preferred_element_type=F32)
        m, num, den = _merge_pair(*ms), both[:, :LANES], both[:, LANES:]
        if cfg > 0:
            rows = pl.ds(n * BLK * dil + r, BLK, stride=dil)
            m_refs[cfg - 1][rows, :], num_refs[cfg - 1][rows, :], den_refs[cfg - 1][rows, :] = m, num, den
            return
        rows = pl.ds(n * BLK, BLK)
        others = [ref[rows, :] for ref in m_refs]
        top = functools.reduce(jnp.maximum, others, m)
        w = jnp.exp2(m - top)
        num, den = w * num, w * den
        for m_c, num_ref, den_ref in zip(others, num_refs, den_refs):
            w = jnp.exp2(m_c - top)
            num, den = num + w * num_ref[rows, :], den + w * den_ref[rows, :]
        o_ref[rows, :] = (num / den).astype(o_ref.dtype)

    assert _DILS[0] == 1
    for cfg in reversed(range(N_CFG)):
        dil = _DILS[cfg]
        for r in range(dil):
            for n in range(DIL_BLOCKS // dil):
                attend(cfg, dil, r, n)
    for cfg, dil in enumerate(_DILS):
        last = DIL_CHUNK // dil - BLK
        prev_k_refs[cfg][...] = k_refs[cfg][:, last:last + BLK, :]
        prev_v_refs[cfg][...] = v_refs[cfg][:, last:last + BLK, :]


def _dilated(qkv, bias_tabs):
    b, _, t, _ = qkv[0].shape
    assert t % DIL_CHUNK == 0
    chunk = lambda d, first_col: pl.BlockSpec((None, d, DIL_CHUNK // d, LANES),
                                              lambda bi, p, c: (bi, 0, c, first_col + p))
    q_specs = [chunk(d, 0) for d in _DILS]
    k_specs = [chunk(d, PAIRS_A) for d in _DILS]
    v_specs = [chunk(d, 2 * PAIRS_A) for d in _DILS]
    carried = [pltpu.VMEM((d, BLK, LANES), BF16) for d in _DILS]
    bias_spec = pl.BlockSpec((N_CFG, 2, 2, BLK, 2 * BLK), lambda bi, p, c: (0, 0, p, 0, 0))
    ones = _head_rows(2 * BLK)
    return pl.pallas_call(
        _dilated_kernel,
        grid=(b, PAIRS_A, t // DIL_CHUNK),
        in_specs=q_specs + k_specs + v_specs
                 + [bias_spec, pl.BlockSpec((4 * BLK, LANES), lambda bi, p, c: (0, 0))],
        out_specs=pl.BlockSpec((None, DIL_CHUNK, LANES), lambda bi, p, c: (bi, c, p)),
        out_shape=jax.ShapeDtypeStruct((b, t, WIDTH_A), BF16),
        scratch_shapes=carried + carried + [pltpu.VMEM((DIL_CHUNK, LANES), F32)] * (3 * (N_CFG - 1)),
        compiler_params=_params(("parallel", "parallel", "arbitrary")),
        name="dilated",
    )(*qkv, *qkv, *qkv, bias_tabs, ones)


def _softplus(z):
    return jnp.maximum(z, 0.0) + jnp.log(1.0 + jnp.exp2(jnp.abs(z) * (-LOG2E)))


def _sb_kernel(q_ref, k_ref, v_ref, tri_ref, heads_ref, o_ref, qs_ref, acc_ref, carry_ref, live_ref):
    step_id = pl.program_id(2)
    bq = SB_BLOCK
    for s in range(SB_SUBS):
        q = q_ref[s * bq:(s + 1) * bq, :]
        qs_ref[s] = jnp.concatenate([q, q], axis=0) * heads_ref[...]
    tri = tri_ref[...]

    def step(qs, j, carry):
        diagonal = carry is None
        ks = pl.multiple_of(j * bq, bq)
        kb = k_ref[pl.ds(ks, bq), :]
        vb = v_ref[pl.ds(ks, bq), :]
        z = lax.dot_general(qs, kb, _NT, preferred_element_type=F32)
        t = _softplus(z)
        if diagonal:
            row = lax.broadcasted_iota(jnp.int32, z.shape, 0) & (bq - 1)
            colv = lax.broadcasted_iota(jnp.int32, z.shape, 1)
            causal = colv < row
            t = jnp.where(causal, t, 0.0)
        after = jnp.dot(t.astype(BF16), tri, preferred_element_type=F32)
        expo = (z - t) + after
        if not diagonal:
            expo = expo + jnp.concatenate([carry] * (bq // LANES), axis=1)
        a = jnp.exp2(expo * LOG2E)
        if diagonal:
            a = jnp.where(causal, a, 0.0)
        pv = jnp.dot(a.astype(BF16), vb, preferred_element_type=F32)
        block_sum = jnp.broadcast_to(after[:, 0:1] - t[:, 0:1], (2 * bq, LANES))
        return pv, block_sum

    def run(s, first_block, count, fresh):
        qs = qs_ref[s]
        acc, carry = (None, None) if fresh else (acc_ref[s], carry_ref[s])
        for u in range(count):
            pv, block_sum = step(qs, first_block - u, carry)
            acc = pv if acc is None else acc + pv
            carry = block_sum if carry is None else carry + block_sum
        acc_ref[s], carry_ref[s] = acc, carry
        return jnp.max(carry)

    @pl.when(step_id == 0)
    def _():
        run(0, 0, 1, True)
        live_ref[0] = SB_DEAD
        for s in range(1, SB_SUBS):
            live_ref[s] = run(s, s, 2, True)

    @pl.when(step_id > 0)
    def _():
        for s in range(SB_SUBS):
            live_ref[s] = run(s, step_id * SB_SUBS + s, 2, True)

    def older(s, c):
        def more(state):
            j, live = state
            return (j >= 1) & (live > SB_DEAD)

        def pair(state):
            j, _ = state
            return j - 2, run(s, j, 2, False)

        j, live = lax.while_loop(more, pair, (step_id * SB_SUBS + s - 2, live_ref[s]))

        @pl.when((j == 0) & (live > SB_DEAD))
        def _():
            run(s, 0, 1, False)

        return c

    @pl.when(functools.reduce(jnp.maximum, [live_ref[s] for s in range(SB_SUBS)]) > SB_DEAD)
    def _():
        lax.fori_loop(0, SB_SUBS, older, 0)

    for s in range(SB_SUBS):
        o_ref[s * bq:(s + 1) * bq, :] = _merge_pair(acc_ref[s, 0:bq, :], acc_ref[s, bq:2 * bq, :]).astype(o_ref.dtype)


def _stick_breaking(qkv):
    b, t, _ = qkv.shape
    bq = SB_BLOCK
    rows = SB_SUBS * bq
    assert t % rows == 0
    r = np.arange(bq)
    tri = jnp.asarray(np.where(r[:, None] > r[None, :], -1.0, 0.0), dtype=BF16)
    heads = _head_rows(bq)
    q_spec = pl.BlockSpec((None, rows, LANES), lambda bi, p, i: (bi, i, p))
    k_spec = pl.BlockSpec((None, t, LANES), lambda bi, p, i: (bi, 0, PAIRS_B + p))
    v_spec = pl.BlockSpec((None, t, LANES), lambda bi, p, i: (bi, 0, 2 * PAIRS_B + p))
    const = lambda shape: pl.BlockSpec(shape, lambda bi, p, i: (0, 0))
    return pl.pallas_call(
        _sb_kernel,
        grid=(b, PAIRS_B, t // rows),
        in_specs=[q_spec, k_spec, v_spec, const((bq, bq)), const((2 * bq, LANES))],
        out_specs=q_spec,
        out_shape=jax.ShapeDtypeStruct((b, t, WIDTH_B), BF16),
        scratch_shapes=[pltpu.VMEM((SB_SUBS, 2 * bq, LANES), BF16), pltpu.VMEM((SB_SUBS, 2 * bq, LANES), F32),
                        pltpu.VMEM((SB_SUBS, 2 * bq, LANES), F32), pltpu.SMEM((SB_SUBS,), F32)],
        compiler_params=_params(("parallel", "parallel", "arbitrary")),
        name="stick_breaking",
    )(qkv, qkv, qkv, tri, heads)


def _out_kernel(*refs, normalize, project):
    ya_ref, yb_ref, gate_ref, h_ref, *refs = refs
    h = h_ref[...]
    if normalize:
        g0_ref, b0_ref, *refs = refs
        h = _layer_norm_rows(h, g0_ref[...], b0_ref[...])
    w_ref, b_ref, g_ref, beta_ref, *refs = refs
    if project:
        w_next_ref, *refs = refs
    hn_ref, *refs = refs
    half_gate = gate_ref[...].astype(F32)
    silu = half_gate + half_gate * jnp.tanh(half_gate)
    ya = (ya_ref[...].astype(F32) * silu[:, :WIDTH_A]).astype(BF16)
    yb = (yb_ref[...].astype(F32) * silu[:, WIDTH_A:]).astype(BF16)
    out = (jnp.dot(ya, w_ref[0:WIDTH_A, :], preferred_element_type=F32)
           + jnp.dot(yb, w_ref[WIDTH_A:MIX_WIDTH, :], preferred_element_type=F32)
           + b_ref[...])
    y = _layer_norm_rows(DEEPNORM_ALPHA * h + out, g_ref[...], beta_ref[...])
    hn_ref[...] = y
    if project:
        _project_rows(y.astype(BF16), w_next_ref, refs)


def _out_block(ya, yb, gate, h, w_out_bf16, layer, b_out, g, beta, batch, ln0=None, w_in_bf16=None):
    n = h.shape[0]
    project = w_in_bf16 is not None
    tile = ROW_TILE if project else OUT_ROW_TILE
    half = pl.BlockSpec((tile, WIDTH_A), lambda i: (i, 0))
    full = pl.BlockSpec((tile, D_MODEL), lambda i: (i, 0))
    vec = pl.BlockSpec((1, D_MODEL), lambda i: (0, 0))
    as_vec = lambda a: a.reshape(1, D_MODEL)
    ln_args = () if ln0 is None else tuple(as_vec(a) for a in ln0)
    in_specs = [half, half, full, full] + [vec] * len(ln_args) + [
        pl.BlockSpec((None, MIX_WIDTH, D_MODEL), lambda i: (layer, 0, 0), pipeline_mode=pl.Buffered(1)),
        vec, vec, vec]
    args = [ya.reshape(n, WIDTH_A), yb.reshape(n, WIDTH_B), gate, h, *ln_args, w_out_bf16,
            as_vec(b_out), as_vec(g), as_vec(beta)]
    out_specs, out_shapes, scratch = [full], [jax.ShapeDtypeStruct((n, D_MODEL), F32)], []
    if not project and ln0 is None:
        deep = lambda spec: pl.BlockSpec(spec.block_shape, spec.index_map, pipeline_mode=pl.Buffered(OUT_BUFFERS))
        whole = pl.BlockSpec((MIX_WIDTH, D_MODEL), lambda i: (0, 0))

        def stream(*refs):
            *ins, w_ref, b_ref, g_ref, beta_ref, o_ref = refs
            pltpu.emit_pipeline(
                functools.partial(_out_kernel, normalize=False, project=False),
                grid=(n // tile,),
                in_specs=[deep(half), deep(half), deep(full), deep(full), whole, vec, vec, vec],
                out_specs=[full],
            )(*ins, w_ref.at[layer], b_ref, g_ref, beta_ref, o_ref)

        hbm = pl.BlockSpec(memory_space=pl.ANY)
        out = pl.pallas_call(
            stream,
            in_specs=[hbm] * len(args),
            out_specs=hbm,
            out_shape=out_shapes[0],
            compiler_params=pltpu.CompilerParams(vmem_limit_bytes=VMEM_LIMIT),
            name="out_proj_norm",
        )(*args)
        return out, None
    if project:
        weights, proj_specs, proj_shapes, scratch = _projection_plumbing(n, batch, layer + 1)
        in_specs, args = in_specs + [weights], args + [w_in_bf16]
        out_specs, out_shapes = out_specs + proj_specs, out_shapes + proj_shapes
    outs = pl.pallas_call(
        functools.partial(_out_kernel, normalize=ln0 is not None, project=project),
        grid=(n // tile,),
        in_specs=in_specs,
        out_specs=out_specs,
        out_shape=out_shapes,
        scratch_shapes=scratch,
        compiler_params=_params(("parallel",)),
        name="out_proj_norm",
    )(*args)
    if not project:
        return outs[0], None
    return outs[0], (outs[1:1 + N_CFG], outs[1 + N_CFG], outs[2 + N_CFG])


def kernel(x, ln0_g, ln0_b, w_in, w_out, b_out, ln_g, ln_b, rel_bias):
    b, t, _ = x.shape
    n = b * t
    bias_tabs = _bias_tables(rel_bias)
    w_in, w_out = w_in.astype(BF16), w_out.astype(BF16)
    h = x.reshape(n, D_MODEL)
    ln0 = (ln0_g, ln0_b)
    projections = _project(h, w_in, b, ln0)
    for l in range(DEPTH):
        qkv_a, qkv_b, gate = projections
        yb = _stick_breaking(qkv_b.reshape(b, t, 3 * WIDTH_B))
        ya = _dilated(qkv_a, bias_tabs)
        h, projections = _out_block(ya, yb, gate, h, w_out, l, b_out[l], ln_g[l], ln_b[l], b, ln0=ln0,
                                    w_in_bf16=w_in if l + 1 < DEPTH else None)
        ln0 = None
    return h.reshape(b, t, D_MODEL)
```

```python
import functools
import math

import numpy as np
import jax
import jax.numpy as jnp
from jax import lax
from jax.experimental import pallas as pl
from jax.experimental.pallas import tpu as pltpu

D_MODEL = 1024
DEPTH = 2
HEAD_DIM = 64
N_HEADS_A = 8
N_HEADS_B = 8
WIDTH_A = N_HEADS_A * HEAD_DIM
WIDTH_B = N_HEADS_B * HEAD_DIM
MIX_WIDTH = WIDTH_A + WIDTH_B
IN_COLS = 3 * WIDTH_A + 3 * WIDTH_B + MIX_WIDTH
DILATED_CONFIGS = ((128, 1), (512, 4), (2048, 16))
N_CFG = len(DILATED_CONFIGS)
BLK = 128
N_BUCKETS = 32
MAX_DISTANCE = 2048
LN_EPS = 1e-5
DEEPNORM_ALPHA = (2.0 * DEPTH) ** 0.25
QK_SCALE = 1.0 / math.sqrt(HEAD_DIM)
LOG2E = math.log2(math.e)

LANES = 128
BF16_ROWS = 16
PAIRS_A = WIDTH_A // LANES
PAIRS_B = WIDTH_B // LANES
NEG_BIG = -1e30

ROW_TILE = 512
OUT_ROW_TILE = 1024
OUT_BUFFERS = 3
SB_BLOCK = 256
SB_SUBS = 32
SB_DEAD = -110.0
DIL_CHUNK = 2 * BLK * max(d for _, d in DILATED_CONFIGS)
DIL_BLOCKS = DIL_CHUNK // BLK
VMEM_LIMIT = 48 * 1024 * 1024
SB_VMEM_LIMIT = 60 * 1024 * 1024

F32 = jnp.float32
BF16 = jnp.bfloat16


def _params(semantics):
    return pltpu.CompilerParams(dimension_semantics=semantics, vmem_limit_bytes=VMEM_LIMIT)


def _layer_norm_rows(x, g, b):
    mu = jnp.mean(x, axis=-1, keepdims=True)
    xc = x - mu
    var = jnp.mean(xc * xc, axis=-1, keepdims=True)
    return xc * lax.rsqrt(var + LN_EPS) * g + b


_PROJ_A = ((0, QK_SCALE * LOG2E), (WIDTH_A, 1.0), (2 * WIDTH_A, 1.0))
_PROJ_B = ((3 * WIDTH_A, QK_SCALE), (3 * WIDTH_A + WIDTH_B, 1.0), (3 * WIDTH_A + 2 * WIDTH_B, 1.0))
_GATE_COL = 3 * WIDTH_A + 3 * WIDTH_B
_DILS = tuple(d for _, d in DILATED_CONFIGS)


def _proj_kernel(x_ref, g_ref, b_ref, w_ref, *refs):
    _project_rows(_layer_norm_rows(x_ref[...], g_ref[...], b_ref[...]).astype(BF16), w_ref, refs)


def _project_rows(h, w_ref, refs):
    a_refs, (b_ref, gate_ref), fold_refs = refs[:N_CFG], refs[N_CFG:N_CFG + 2], refs[N_CFG + 2:]
    slabs = range(WIDTH_A // LANES)
    for t, (c0, scale) in enumerate(_PROJ_A):
        cols = slice(t * WIDTH_A, (t + 1) * WIDTH_A)
        acc = jnp.dot(h, w_ref[:, c0:c0 + WIDTH_A], preferred_element_type=F32)
        if scale != 1.0:
            acc = acc * scale
        a_refs[0][0, :, cols] = acc.astype(BF16)
        src_ref, src_dil = fold_refs[2 * t], 1
        for j in slabs:
            src_ref[j] = acc[:, j * LANES:(j + 1) * LANES]
        for cfg in range(1, N_CFG):
            dil, dst_ref = _DILS[cfg], fold_refs[2 * t + cfg % 2]
            step, rows_out = dil // src_dil, ROW_TILE // dil
            for r_src in range(src_dil):
                for a in range(step):
                    r = r_src + src_dil * a
                    rows = pl.ds(r_src * (ROW_TILE // src_dil) + a, rows_out, stride=step)
                    parts = [src_ref[j, rows, :] for j in slabs]
                    a_refs[cfg][r, :, cols] = jnp.concatenate(parts, axis=1).astype(BF16)
                    if cfg + 1 < N_CFG:
                        for j in slabs:
                            dst_ref[j, r * rows_out:(r + 1) * rows_out, :] = parts[j]
            src_ref, src_dil = dst_ref, dil
    for t, (c0, scale) in enumerate(_PROJ_B):
        acc = jnp.dot(h, w_ref[:, c0:c0 + WIDTH_B], preferred_element_type=F32)
        if scale != 1.0:
            acc = acc * scale
        b_ref[:, t * WIDTH_B:(t + 1) * WIDTH_B] = acc.astype(BF16)
    gate = jnp.dot(h, w_ref[:, _GATE_COL:_GATE_COL + MIX_WIDTH], preferred_element_type=F32)
    gate_ref[...] = (0.5 * gate).astype(BF16)


def _projection_plumbing(n, batch, layer):
    t = n // batch
    tiles = t // ROW_TILE
    assert t % ROW_TILE == 0 and all(ROW_TILE % (BF16_ROWS * d) == 0 for d in _DILS)
    weights = pl.BlockSpec((None, D_MODEL, IN_COLS), lambda i: (layer, 0, 0),
                           pipeline_mode=pl.Buffered(1))
    a_specs = [pl.BlockSpec((None, d, ROW_TILE // d, 3 * WIDTH_A), lambda i: (i // tiles, 0, i % tiles, 0))
               for d in _DILS]
    a_shapes = [jax.ShapeDtypeStruct((batch, d, t // d, 3 * WIDTH_A), BF16) for d in _DILS]
    b_widths = (3 * WIDTH_B, MIX_WIDTH)
    out_specs = a_specs + [pl.BlockSpec((ROW_TILE, w), lambda i: (i, 0)) for w in b_widths]
    out_shapes = a_shapes + [jax.ShapeDtypeStruct((n, w), BF16) for w in b_widths]
    scratch = [pltpu.VMEM((WIDTH_A // LANES, ROW_TILE, LANES), F32)] * (2 * len(_PROJ_A))
    return weights, out_specs, out_shapes, scratch


def _project(x2, w_bf16, batch, ln0):
    n = x2.shape[0]
    row = pl.BlockSpec((ROW_TILE, D_MODEL), lambda i: (i, 0))
    vec = pl.BlockSpec((1, D_MODEL), lambda i: (0, 0))
    weights, out_specs, out_shapes, scratch = _projection_plumbing(n, batch, 0)
    outs = pl.pallas_call(
        _proj_kernel,
        grid=(n // ROW_TILE,),
        in_specs=[row, vec, vec, weights],
        out_specs=out_specs,
        out_shape=out_shapes,
        scratch_shapes=scratch,
        compiler_params=_params(("parallel",)),
        name="in_proj",
    )(x2, *(a.reshape(1, D_MODEL) for a in ln0), w_bf16)
    return outs[:N_CFG], outs[N_CFG], outs[N_CFG + 1]


def _head_rows(rows_per_head):
    head_of_lane = np.arange(LANES) // HEAD_DIM
    head_of_row = np.arange(2 * rows_per_head) // rows_per_head
    return jnp.asarray(head_of_row[:, None] == head_of_lane[None, :], dtype=BF16)


def _merge_pair(x0, x1):
    lane = lax.broadcasted_iota(jnp.int32, x0.shape, 1)
    return jnp.where(lane < HEAD_DIM, x0, x1)


_NT = (((1,), (1,)), ((), ()))


def _t5_bucket(dist):
    max_exact = N_BUCKETS // 2
    n = np.maximum(dist, 1).astype(np.float32)
    large = max_exact + (np.log(n / max_exact) / math.log(MAX_DISTANCE / max_exact)
                         * (N_BUCKETS - max_exact)).astype(np.int32)
    large = np.minimum(large, N_BUCKETS - 1)
    return np.where(dist < max_exact, dist, large).astype(np.int32)


def _bucket_segments(window, dil):
    buckets = _t5_bucket(np.arange(window // dil + 1) * dil)
    return [(d, int(bk)) for d, bk in enumerate(buckets) if d == 0 or bk != buckets[d - 1]]


def _bias_kernel(rel_ref, out_ref):
    i = lax.broadcasted_iota(jnp.int32, (BLK, 2 * BLK), 0)
    j = lax.broadcasted_iota(jnp.int32, (BLK, 2 * BLK), 1)
    delta = BLK + i - j
    for cfg, (window, dil) in enumerate(DILATED_CONFIGS):
        segments = _bucket_segments(window, dil)
        for variant in range(2):
            band = (delta >= 0) & (delta <= window // dil) & (j >= variant * BLK)
            steps = [delta >= start for start, _ in segments[1:]]
            for h in range(N_HEADS_A):
                val = jnp.full((BLK, 2 * BLK), rel_ref[segments[0][1], h] * LOG2E, F32)
                for step, (_, bucket) in zip(steps, segments[1:]):
                    val = jnp.where(step, rel_ref[bucket, h] * LOG2E, val)
                out_ref[cfg, variant, h] = jnp.where(band, val, NEG_BIG)


def _bias_tables(rel_bias):
    shape = (N_CFG, 2, N_HEADS_A, BLK, 2 * BLK)
    return pl.pallas_call(
        _bias_kernel,
        in_specs=[pl.BlockSpec(memory_space=pltpu.SMEM)],
        out_shape=jax.ShapeDtypeStruct(shape, F32),
        compiler_params=pltpu.CompilerParams(vmem_limit_bytes=VMEM_LIMIT),
        name="bias_tables",
    )(rel_bias.astype(F32))


def _dilated_kernel(*refs):
    q_refs, k_refs, v_refs = refs[:N_CFG], refs[N_CFG:2 * N_CFG], refs[2 * N_CFG:3 * N_CFG]
    bias_ref, ones_ref, o_ref, *scratch_refs = refs[3 * N_CFG:]
    prev_k_refs, prev_v_refs, state_refs = scratch_refs[:N_CFG], scratch_refs[N_CFG:2 * N_CFG], scratch_refs[2 * N_CFG:]
    c = pl.program_id(2)
    n_state = N_CFG - 1
    m_refs, den_refs, num_refs = state_refs[:n_state], state_refs[n_state:2 * n_state], state_refs[2 * n_state:]

    @pl.when(c == 0)
    def _():
        for cfg in range(N_CFG):
            prev_k_refs[cfg][...] = k_refs[cfg][:, 0:BLK, :]
            prev_v_refs[cfg][...] = v_refs[cfg][:, 0:BLK, :]

    def key_rows(ref, prev_ref, r, n):
        if n == 0:
            return jnp.concatenate([prev_ref[r], ref[r, 0:BLK, :]], axis=0)
        return ref[r, (n - 1) * BLK:(n + 1) * BLK, :]

    def attend(cfg, dil, r, n):
        first = jnp.where(c == 0, 1, 0) if n == 0 else 0
        ones = ones_ref[...]
        q = q_refs[cfg][r, n * BLK:(n + 1) * BLK, :]
        q_pair = (q * ones[0:BLK], q * ones[2 * BLK:3 * BLK])
        kb = key_rows(k_refs[cfg], prev_k_refs[cfg], r, n)
        vb = key_rows(v_refs[cfg], prev_v_refs[cfg], r, n)
        v_sum = jnp.concatenate([jnp.concatenate([vb, vb], axis=0) * ones, ones], axis=1)
        ms, ps = [], []
        for h in range(2):
            s = lax.dot_general(q_pair[h], kb, _NT, preferred_element_type=F32) + bias_ref[cfg, first, h]
            m = jnp.max(s, axis=-1, keepdims=True)
            ps.append(jnp.exp2(s - m).astype(BF16))
            ms.append(jnp.broadcast_to(m, (BLK, LANES)))
        both = jnp.dot(jnp.concatenate(ps, axis=1), v_sum, preferred_element_type=F32)
        m, num, den = _merge_pair(*ms), both[:, :LANES], both[:, LANES:]
        if cfg > 0:
            rows = pl.ds(n * BLK * dil + r, BLK, stride=dil)
            m_refs[cfg - 1][rows, :], num_refs[cfg - 1][rows, :], den_refs[cfg - 1][rows, :] = m, num, den
            return
        rows = pl.ds(n * BLK, BLK)
        others = [ref[rows, :] for ref in m_refs]
        top = functools.reduce(jnp.maximum, others, m)
        w = jnp.exp2(m - top)
        num, den = w * num, w * den
        for m_c, num_ref, den_ref in zip(others, num_refs, den_refs):
            w = jnp.exp2(m_c - top)
            num, den = num + w * num_ref[rows, :], den + w * den_ref[rows, :]
        o_ref[rows, :] = (num / den).astype(o_ref.dtype)

    assert _DILS[0] == 1
    for cfg in reversed(range(N_CFG)):
        dil = _DILS[cfg]
        for r in range(dil):
            for n in range(DIL_BLOCKS // dil):
                attend(cfg, dil, r, n)
    for cfg, dil in enumerate(_DILS):
        last = DIL_CHUNK // dil - BLK
        prev_k_refs[cfg][...] = k_refs[cfg][:, last:last + BLK, :]
        prev_v_refs[cfg][...] = v_refs[cfg][:, last:last + BLK, :]


def _dilated(qkv, bias_tabs):
    b, _, t, _ = qkv[0].shape
    assert t % DIL_CHUNK == 0
    chunk = lambda d, first_col: pl.BlockSpec((None, d, DIL_CHUNK // d, LANES),
                                              lambda bi, p, c: (bi, 0, c, first_col + p))
    q_specs = [chunk(d, 0) for d in _DILS]
    k_specs = [chunk(d, PAIRS_A) for d in _DILS]
    v_specs = [chunk(d, 2 * PAIRS_A) for d in _DILS]
    carried = [pltpu.VMEM((d, BLK, LANES), BF16) for d in _DILS]
    bias_spec = pl.BlockSpec((N_CFG, 2, 2, BLK, 2 * BLK), lambda bi, p, c: (0, 0, p, 0, 0))
    ones = _head_rows(2 * BLK)
    return pl.pallas_call(
        _dilated_kernel,
        grid=(b, PAIRS_A, t // DIL_CHUNK),
        in_specs=q_specs + k_specs + v_specs
                 + [bias_spec, pl.BlockSpec((4 * BLK, LANES), lambda bi, p, c: (0, 0))],
        out_specs=pl.BlockSpec((None, DIL_CHUNK, LANES), lambda bi, p, c: (bi, c, p)),
        out_shape=jax.ShapeDtypeStruct((b, t, WIDTH_A), BF16),
        scratch_shapes=carried + carried + [pltpu.VMEM((DIL_CHUNK, LANES), F32)] * (3 * (N_CFG - 1)),
        compiler_params=_params(("parallel", "parallel", "arbitrary")),
        name="dilated",
    )(*qkv, *qkv, *qkv, bias_tabs, ones)


def _softplus(z):
    return jnp.maximum(z, 0.0) + jnp.log(1.0 + jnp.exp2(jnp.abs(z) * (-LOG2E)))


def _sb_kernel(q_ref, k_ref, v_ref, tri_ref, heads_ref, o_ref, qs_ref, acc_ref, carry_ref, live_ref, step=None):
    step_id = pl.program_id(2) if step is None else jnp.int32(step)
    bq = SB_BLOCK
    for s in range(SB_SUBS):
        q = q_ref[s * bq:(s + 1) * bq, :]
        qs_ref[s] = jnp.concatenate([q, q], axis=0) * heads_ref[...]
    tri = tri_ref[...]

    def step(qs, j, carry):
        diagonal = carry is None
        ks = pl.multiple_of(j * bq, bq)
        kb = k_ref[pl.ds(ks, bq), :]
        vb = v_ref[pl.ds(ks, bq), :]
        z = lax.dot_general(qs, kb, _NT, preferred_element_type=F32)
        t = _softplus(z)
        if diagonal:
            row = lax.broadcasted_iota(jnp.int32, z.shape, 0) & (bq - 1)
            colv = lax.broadcasted_iota(jnp.int32, z.shape, 1)
            causal = colv < row
            t = jnp.where(causal, t, 0.0)
        after = jnp.dot(t.astype(BF16), tri, preferred_element_type=F32)
        expo = (z - t) + after
        if not diagonal:
            expo = expo + jnp.concatenate([carry] * (bq // LANES), axis=1)
        a = jnp.exp2(expo * LOG2E)
        if diagonal:
            a = jnp.where(causal, a, 0.0)
        pv = jnp.dot(a.astype(BF16), vb, preferred_element_type=F32)
        block_sum = jnp.broadcast_to(after[:, 0:1] - t[:, 0:1], (2 * bq, LANES))
        return pv, block_sum

    def run(s, first_block, count, fresh):
        qs = qs_ref[s]
        acc, carry = (None, None) if fresh else (acc_ref[s], carry_ref[s])
        for u in range(count):
            pv, block_sum = step(qs, first_block - u, carry)
            acc = pv if acc is None else acc + pv
            carry = block_sum if carry is None else carry + block_sum
        acc_ref[s], carry_ref[s] = acc, carry
        return jnp.max(carry)

    @pl.when(step_id == 0)
    def _():
        run(0, 0, 1, True)
        live_ref[0] = SB_DEAD
        for s in range(1, SB_SUBS):
            live_ref[s] = run(s, s, 2, True)

    @pl.when(step_id > 0)
    def _():
        for s in range(SB_SUBS):
            live_ref[s] = run(s, step_id * SB_SUBS + s, 2, True)

    def older(s, c):
        def more(state):
            j, live = state
            return (j >= 1) & (live > SB_DEAD)

        def pair(state):
            j, _ = state
            return j - 2, run(s, j, 2, False)

        j, live = lax.while_loop(more, pair, (step_id * SB_SUBS + s - 2, live_ref[s]))

        @pl.when((j == 0) & (live > SB_DEAD))
        def _():
            run(s, 0, 1, False)

        return c

    @pl.when(functools.reduce(jnp.maximum, [live_ref[s] for s in range(SB_SUBS)]) > SB_DEAD)
    def _():
        lax.fori_loop(0, SB_SUBS, older, 0)

    for s in range(SB_SUBS):
        o_ref[s * bq:(s + 1) * bq, :] = _merge_pair(acc_ref[s, 0:bq, :], acc_ref[s, bq:2 * bq, :]).astype(o_ref.dtype)


def _stick_breaking(qkv):
    b, t, _ = qkv.shape
    bq = SB_BLOCK
    rows = SB_SUBS * bq
    assert t % rows == 0
    r = np.arange(bq)
    tri = jnp.asarray(np.where(r[:, None] > r[None, :], -1.0, 0.0), dtype=BF16)
    heads = _head_rows(bq)
    q_spec = pl.BlockSpec((None, rows, LANES), lambda bi, p, i: (bi, i, p))
    k_spec = pl.BlockSpec((None, t, LANES), lambda bi, p, i: (bi, 0, PAIRS_B + p))
    v_spec = pl.BlockSpec((None, t, LANES), lambda bi, p, i: (bi, 0, 2 * PAIRS_B + p))
    scratch = [pltpu.VMEM((SB_SUBS, 2 * bq, LANES), BF16), pltpu.VMEM((SB_SUBS, 2 * bq, LANES), F32),
               pltpu.VMEM((SB_SUBS, 2 * bq, LANES), F32), pltpu.SMEM((SB_SUBS,), F32)]
    if t == rows:
        col = lambda first: pl.BlockSpec((None, t, LANES), lambda bi, p: (bi, 0, first + p))
        fixed = lambda shape: pl.BlockSpec(shape, lambda bi, p: (0, 0))

        def stream(q_hbm, k_hbm, v_hbm, tri_hbm, heads_hbm, o_hbm, *scratch_refs):
            def body(*refs):
                _sb_kernel(*refs, *scratch_refs, step=0)

            pltpu.emit_pipeline(
                body, grid=(b, PAIRS_B),
                in_specs=[col(0), col(PAIRS_B), col(2 * PAIRS_B), fixed((bq, bq)), fixed((2 * bq, LANES))],
                out_specs=[col(0)],
            )(q_hbm, k_hbm, v_hbm, tri_hbm, heads_hbm, o_hbm)

        hbm = pl.BlockSpec(memory_space=pl.ANY)
        return pl.pallas_call(
            stream,
            in_specs=[hbm] * 5,
            out_specs=hbm,
            out_shape=jax.ShapeDtypeStruct((b, t, WIDTH_B), BF16),
            scratch_shapes=scratch,
            compiler_params=pltpu.CompilerParams(vmem_limit_bytes=SB_VMEM_LIMIT),
            name="stick_breaking",
        )(qkv, qkv, qkv, tri, heads)
    const = lambda shape: pl.BlockSpec(shape, lambda bi, p, i: (0, 0))
    return pl.pallas_call(
        _sb_kernel,
        grid=(b, PAIRS_B, t // rows),
        in_specs=[q_spec, k_spec, v_spec, const((bq, bq)), const((2 * bq, LANES))],
        out_specs=q_spec,
        out_shape=jax.ShapeDtypeStruct((b, t, WIDTH_B), BF16),
        scratch_shapes=scratch,
        compiler_params=pltpu.CompilerParams(dimension_semantics=("parallel", "parallel", "arbitrary"),
                                             vmem_limit_bytes=SB_VMEM_LIMIT),
        name="stick_breaking",
    )(qkv, qkv, qkv, tri, heads)


def _out_kernel(*refs, normalize, project):
    ya_ref, yb_ref, gate_ref, h_ref, *refs = refs
    h = h_ref[...]
    if normalize:
        g0_ref, b0_ref, *refs = refs
        h = _layer_norm_rows(h, g0_ref[...], b0_ref[...])
    w_ref, b_ref, g_ref, beta_ref, *refs = refs
    if project:
        w_next_ref, *refs = refs
    hn_ref, *refs = refs
    half_gate = gate_ref[...].astype(F32)
    silu = half_gate + half_gate * jnp.tanh(half_gate)
    ya = (ya_ref[...].astype(F32) * silu[:, :WIDTH_A]).astype(BF16)
    yb = (yb_ref[...].astype(F32) * silu[:, WIDTH_A:]).astype(BF16)
    out = (jnp.dot(ya, w_ref[0:WIDTH_A, :], preferred_element_type=F32)
           + jnp.dot(yb, w_ref[WIDTH_A:MIX_WIDTH, :], preferred_element_type=F32)
           + b_ref[...])
    y = _layer_norm_rows(DEEPNORM_ALPHA * h + out, g_ref[...], beta_ref[...])
    hn_ref[...] = y
    if project:
        _project_rows(y.astype(BF16), w_next_ref, refs)


def _out_block(ya, yb, gate, h, w_out_bf16, layer, b_out, g, beta, batch, ln0=None, w_in_bf16=None):
    n = h.shape[0]
    project = w_in_bf16 is not None
    tile = ROW_TILE if project else OUT_ROW_TILE
    half = pl.BlockSpec((tile, WIDTH_A), lambda i: (i, 0))
    full = pl.BlockSpec((tile, D_MODEL), lambda i: (i, 0))
    vec = pl.BlockSpec((1, D_MODEL), lambda i: (0, 0))
    as_vec = lambda a: a.reshape(1, D_MODEL)
    ln_args = () if ln0 is None else tuple(as_vec(a) for a in ln0)
    in_specs = [half, half, full, full] + [vec] * len(ln_args) + [
        pl.BlockSpec((None, MIX_WIDTH, D_MODEL), lambda i: (layer, 0, 0), pipeline_mode=pl.Buffered(1)),
        vec, vec, vec]
    args = [ya.reshape(n, WIDTH_A), yb.reshape(n, WIDTH_B), gate, h, *ln_args, w_out_bf16,
            as_vec(b_out), as_vec(g), as_vec(beta)]
    out_specs, out_shapes, scratch = [full], [jax.ShapeDtypeStruct((n, D_MODEL), F32)], []
    if not project and ln0 is None:
        deep = lambda spec: pl.BlockSpec(spec.block_shape, spec.index_map, pipeline_mode=pl.Buffered(OUT_BUFFERS))
        whole = pl.BlockSpec((MIX_WIDTH, D_MODEL), lambda i: (0, 0))

        def stream(*refs):
            *ins, w_ref, b_ref, g_ref, beta_ref, o_ref = refs
            pltpu.emit_pipeline(
                functools.partial(_out_kernel, normalize=False, project=False),
                grid=(n // tile,),
                in_specs=[deep(half), deep(half), deep(full), deep(full), whole, vec, vec, vec],
                out_specs=[full],
            )(*ins, w_ref.at[layer], b_ref, g_ref, beta_ref, o_ref)

        hbm = pl.BlockSpec(memory_space=pl.ANY)
        out = pl.pallas_call(
            stream,
            in_specs=[hbm] * len(args),
            out_specs=hbm,
            out_shape=out_shapes[0],
            compiler_params=pltpu.CompilerParams(vmem_limit_bytes=VMEM_LIMIT),
            name="out_proj_norm",
        )(*args)
        return out, None
    if project:
        weights, proj_specs, proj_shapes, scratch = _projection_plumbing(n, batch, layer + 1)
        in_specs, args = in_specs + [weights], args + [w_in_bf16]
        out_specs, out_shapes = out_specs + proj_specs, out_shapes + proj_shapes
    outs = pl.pallas_call(
        functools.partial(_out_kernel, normalize=ln0 is not None, project=project),
        grid=(n // tile,),
        in_specs=in_specs,
        out_specs=out_specs,
        out_shape=out_shapes,
        scratch_shapes=scratch,
        compiler_params=_params(("parallel",)),
        name="out_proj_norm",
    )(*args)
    if not project:
        return outs[0], None
    return outs[0], (outs[1:1 + N_CFG], outs[1 + N_CFG], outs[2 + N_CFG])


def kernel(x, ln0_g, ln0_b, w_in, w_out, b_out, ln_g, ln_b, rel_bias):
    b, t, _ = x.shape
    n = b * t
    bias_tabs = _bias_tables(rel_bias)
    w_in, w_out = w_in.astype(BF16), w_out.astype(BF16)
    h = x.reshape(n, D_MODEL)
    ln0 = (ln0_g, ln0_b)
    projections = _project(h, w_in, b, ln0)
    for l in range(DEPTH):
        qkv_a, qkv_b, gate = projections
        yb = _stick_breaking(qkv_b.reshape(b, t, 3 * WIDTH_B))
        ya = _dilated(qkv_a, bias_tabs)
        h, projections = _out_block(ya, yb, gate, h, w_out, l, b_out[l], ln_g[l], ln_b[l], b, ln0=ln0,
                                    w_in_bf16=w_in if l + 1 < DEPTH else None)
        ln0 = None
    return h.reshape(b, t, D_MODEL)
```
